```python
import jax, jax.numpy as jnp
from jax import lax
import numpy as np

D_MODEL = 1024
BATCH = 8
SEQ = 4096
DEPTH = 2

GRID_W = 64
CTX_LEN = 256
D_CONV = 512
CONV_K = 31
D_SHORT = 512
SHORT_K = 3
N_Q_HEADS = 8
N_KV_HEADS = 2
HEAD_DIM = 64
WINDOW = 128
BLOCK = 128
D_FOURIER = 512
N_FOURIER_GROUPS = 4
N_BRANCH = 4
N_EXPERTS = 16
CAPACITY_FACTOR = 2
D_EXPERT = 1024
ROPE_BASE = 10000.0
EPS = 1e-6
NEG_INF = -1e30
N_IN = 2 * D_CONV + 3 * D_SHORT + (N_Q_HEADS + 2 * N_KV_HEADS) * HEAD_DIM + D_FOURIER + N_BRANCH * D_MODEL

kernel_name = 'hybrid_gated_parallel_ec_moe_dit'


def rms_norm(x, g):
    xf = x.astype(jnp.float32)
    y = xf * lax.rsqrt(jnp.mean(xf * xf, axis=-1, keepdims=True) + EPS)
    return (y * g.astype(jnp.float32)).astype(x.dtype)


def layer_norm(x, g, b):
    xf = x.astype(jnp.float32)
    mu = jnp.mean(xf, axis=-1, keepdims=True)
    xc = xf - mu
    y = xc * lax.rsqrt(jnp.mean(xc * xc, axis=-1, keepdims=True) + EPS)
    return (y * g.astype(jnp.float32) + b.astype(jnp.float32)).astype(x.dtype)


def modulate(h, shift, scale):
    return h * (1 + scale) + shift


def dwconv(x, w):
    k = w.shape[0]
    return lax.conv_general_dilated(
        x, w[:, None, :].astype(x.dtype), window_strides=(1,),
        padding=[(k // 2, k // 2)], dimension_numbers=('NWC', 'WIO', 'NWC'),
        feature_group_count=x.shape[-1])


def col_offsets():
    sizes = (2 * D_CONV, 3 * D_SHORT, N_Q_HEADS * HEAD_DIM, N_KV_HEADS * HEAD_DIM,
             N_KV_HEADS * HEAD_DIM, D_FOURIER, N_BRANCH * D_MODEL)
    offs = [0]
    for s in sizes:
        offs.append(offs[-1] + s)
    return offs


def split_cols(p):
    o = col_offsets()
    return [p[..., o[i]:o[i + 1]] for i in range(len(o) - 1)]


def to_heads(t, n):
    return t.reshape(t.shape[0], t.shape[1], n, HEAD_DIM)


def axial_rope_tables(length):
    rows = length // GRID_W
    row = jnp.repeat(jnp.arange(rows), GRID_W).astype(jnp.float32)
    col = jnp.tile(jnp.arange(GRID_W), rows).astype(jnp.float32)
    nf = HEAD_DIM // 4
    inv = ROPE_BASE ** (-jnp.arange(nf, dtype=jnp.float32) / nf)
    ar = row[:, None] * inv
    ac = col[:, None] * inv
    return jnp.cos(ar), jnp.sin(ar), jnp.cos(ac), jnp.sin(ac)


def rotate(xp, cs, sn):
    nf = HEAD_DIM // 4
    x1, x2 = xp[..., :nf], xp[..., nf:]
    cs, sn = cs[:, None, :], sn[:, None, :]
    return jnp.concatenate([x1 * cs - x2 * sn, x1 * sn + x2 * cs], axis=-1)


def apply_axial_rope(x, cos_r, sin_r, cos_c, sin_c):
    xf = x.astype(jnp.float32)
    half = HEAD_DIM // 2
    y = jnp.concatenate([rotate(xf[..., :half], cos_r, sin_r),
                         rotate(xf[..., half:], cos_c, sin_c)], axis=-1)
    return y.astype(x.dtype)


def latent_window_attention(qx, kx, vx, kc, vc, sink):
    b, s = qx.shape[0], qx.shape[1]
    nb = s // BLOCK
    g = N_Q_HEADS // N_KV_HEADS
    f32 = jnp.float32
    qb = (qx.astype(f32) * HEAD_DIM ** -0.5).reshape(b, nb, BLOCK, N_KV_HEADS, g, HEAD_DIM)
    pad = ((0, 0), (BLOCK, BLOCK), (0, 0), (0, 0))
    kp = jnp.pad(kx.astype(f32), pad).reshape(b, nb + 2, BLOCK, N_KV_HEADS, HEAD_DIM)
    vp = jnp.pad(vx.astype(f32), pad).reshape(b, nb + 2, BLOCK, N_KV_HEADS, HEAD_DIM)
    kband = jnp.concatenate([kp[:, :-2], kp[:, 1:-1], kp[:, 2:]], axis=2)
    vband = jnp.concatenate([vp[:, :-2], vp[:, 1:-1], vp[:, 2:]], axis=2)
    kcf, vcf = kc.astype(f32), vc.astype(f32)
    s_loc = jnp.einsum('bnqkgd,bnjkd->bnkgqj', qb, kband)
    qi = jnp.arange(BLOCK)[:, None]
    kj = jnp.arange(3 * BLOCK)[None, :]
    kpos = jnp.arange(nb)[:, None, None] * BLOCK - BLOCK + kj[None]
    valid = (jnp.abs(kj - BLOCK - qi) <= WINDOW)[None] & (kpos >= 0) & (kpos < s)
    s_loc = jnp.where(valid[None, :, None, None], s_loc, NEG_INF)
    s_ctx = jnp.einsum('bnqkgd,bckd->bnkgqc', qb, kcf)
    sink_b = jnp.broadcast_to(sink.astype(f32).reshape(1, 1, N_KV_HEADS, g, 1, 1),
                              s_loc.shape[:-1] + (1,))
    p = jax.nn.softmax(jnp.concatenate([s_loc, s_ctx, sink_b], axis=-1), axis=-1)
    n_ctx = kc.shape[1]
    p_loc = p[..., :3 * BLOCK]
    p_ctx = p[..., 3 * BLOCK:3 * BLOCK + n_ctx]
    o = (jnp.einsum('bnkgqj,bnjkd->bnqkgd', p_loc, vband)
         + jnp.einsum('bnkgqc,bckd->bnqkgd', p_ctx, vcf))
    return o.reshape(b, s, N_Q_HEADS * HEAD_DIM).astype(qx.dtype)


def context_attention(qc, kc, vc, sink):
    b, n = qc.shape[0], qc.shape[1]
    g = N_Q_HEADS // N_KV_HEADS
    f32 = jnp.float32
    q = (qc.astype(f32) * HEAD_DIM ** -0.5).reshape(b, n, N_KV_HEADS, g, HEAD_DIM)
    sc = jnp.einsum('bqkgd,bckd->bkgqc', q, kc.astype(f32))
    sink_b = jnp.broadcast_to(sink.astype(f32).reshape(1, N_KV_HEADS, g, 1, 1), sc.shape[:-1] + (1,))
    p = jax.nn.softmax(jnp.concatenate([sc, sink_b], axis=-1), axis=-1)[..., :n]
    o = jnp.einsum('bkgqc,bckd->bqkgd', p, vc.astype(f32))
    return o.reshape(b, n, N_Q_HEADS * HEAD_DIM).astype(qc.dtype)


def conformer_conv(u, conv_w, conv_b, ln_g, ln_b, w_out):
    a, gt = u[..., :D_CONV], u[..., D_CONV:]
    h = a * jax.nn.sigmoid(gt)
    h = dwconv(h, conv_w) + conv_b
    h = jax.nn.silu(layer_norm(h, ln_g, ln_b))
    return h @ w_out


def short_gated_conv(u, conv_w, w_out):
    bg, cg, hv = u[..., :D_SHORT], u[..., D_SHORT:2 * D_SHORT], u[..., 2 * D_SHORT:]
    return (bg * dwconv(cg * hv, conv_w)) @ w_out


def fourier_mix(u, w_out):
    b, n = u.shape[0], u.shape[1]
    ug = u.astype(jnp.float32).reshape(b, n, N_FOURIER_GROUPS, D_FOURIER // N_FOURIER_GROUPS)
    f = jnp.fft.fft2(ug, axes=(1, 3), norm='ortho').real
    return f.reshape(b, n, D_FOURIER).astype(u.dtype) @ w_out


def merge_branches(branches, gate_logits, gate_b, w_o):
    b, n = gate_logits.shape[0], gate_logits.shape[1]
    gates = jax.nn.sigmoid((gate_logits + gate_b).reshape(b, n, N_BRANCH, D_MODEL))
    y = gates[:, :, 0] * branches[0]
    for i in range(1, N_BRANCH):
        y = y + gates[:, :, i] * branches[i]
    return y @ w_o


def mixer_sublayer(hx, hc, ctx_out, rope, w_in, gate_b, conv_a_w, conv_a_b, ln_a_g, ln_a_b,
                   w_a_out, conv_b_w, w_b_out, sink, w_c_out, w_d_out, w_o):
    ax, bx, qx, kx, vx, fx, gx = split_cols(hx @ w_in)
    o = col_offsets()
    if ctx_out:
        ac, bc, qc, kc, vc, fc, gc = split_cols(hc @ w_in)
    else:
        kvc = hc @ w_in[:, o[3]:o[5]]
        kc, vc = kvc[..., :N_KV_HEADS * HEAD_DIM], kvc[..., N_KV_HEADS * HEAD_DIM:]
    kc, vc = to_heads(kc, N_KV_HEADS), to_heads(vc, N_KV_HEADS)
    qx = apply_axial_rope(to_heads(qx, N_Q_HEADS), *rope)
    kx = apply_axial_rope(to_heads(kx, N_KV_HEADS), *rope)
    att_x = latent_window_attention(qx, kx, to_heads(vx, N_KV_HEADS), kc, vc, sink)
    yx = merge_branches([conformer_conv(ax, conv_a_w, conv_a_b, ln_a_g, ln_a_b, w_a_out),
                         short_gated_conv(bx, conv_b_w, w_b_out),
                         att_x @ w_c_out,
                         fourier_mix(fx, w_d_out)], gx, gate_b, w_o)
    if not ctx_out:
        return yx, None
    att_c = context_attention(to_heads(qc, N_Q_HEADS), kc, vc, sink)
    yc = merge_branches([conformer_conv(ac, conv_a_w, conv_a_b, ln_a_g, ln_a_b, w_a_out),
                         short_gated_conv(bc, conv_b_w, w_b_out),
                         att_c @ w_c_out,
                         fourier_mix(fc, w_d_out)], gc, gate_b, w_o)
    return yx, yc


def expert_choice_moe(h, router_w, w1, w3, w2):
    b, n, _ = h.shape
    cap = CAPACITY_FACTOR * n // N_EXPERTS
    aff = jax.nn.softmax((h @ router_w).astype(jnp.float32), axis=-1)
    vals, idx = lax.top_k(jnp.swapaxes(aff, 1, 2), cap)
    bidx = jnp.arange(b)[:, None, None]
    xg = h[bidx, idx]
    hid = jax.nn.silu(jnp.einsum('becd,edf->becf', xg, w1)) * jnp.einsum('becd,edf->becf', xg, w3)
    y = jnp.einsum('becf,efd->becd', hid, w2) * vals[..., None].astype(h.dtype)
    return jnp.zeros_like(h).at[bidx, idx].add(y)


def setup_inputs(seed: int = 0) -> dict:
    key = jax.random.key(seed)
    ks = jax.random.split(key, 32)
    f32 = jnp.float32

    def nrm(k, shape, scale):
        return jax.random.normal(k, shape, f32) * scale

    def gain(k, shape):
        return 1.0 + 0.05 * jax.random.normal(k, shape, f32)

    L, D = DEPTH, D_MODEL
    return {
        'x': nrm(ks[0], (BATCH, SEQ, D), 1.0),
        'c': nrm(ks[1], (BATCH, D), 1.0),
        'ctx': nrm(ks[2], (BATCH, CTX_LEN, D), 1.0),
        'c_ctx': nrm(ks[3], (D,), 1.0),
        'ada_w': nrm(ks[4], (L, D, 6 * D), 0.5 * D ** -0.5),
        'ada_b': nrm(ks[5], (L, 6 * D), 0.02),
        'pre_mix_g': gain(ks[6], (L, D)),
        'post_mix_g': gain(ks[7], (L, D)),
        'pre_ffn_g': gain(ks[8], (L, D)),
        'post_ffn_g': gain(ks[9], (L, D)),
        'w_in': nrm(ks[10], (L, D, N_IN), D ** -0.5),
        'gate_b': nrm(ks[11], (L, N_BRANCH * D), 0.1),
        'conv_a_w': nrm(ks[12], (L, CONV_K, D_CONV), CONV_K ** -0.5),
        'conv_a_b': nrm(ks[13], (L, D_CONV), 0.02),
        'ln_a_g': gain(ks[14], (L, D_CONV)),
        'ln_a_b': nrm(ks[15], (L, D_CONV), 0.02),
        'w_a_out': nrm(ks[16], (L, D_CONV, D), D_CONV ** -0.5),
        'conv_b_w': nrm(ks[17], (L, SHORT_K, D_SHORT), SHORT_K ** -0.5),
        'w_b_out': nrm(ks[18], (L, D_SHORT, D), D_SHORT ** -0.5),
        'sink': nrm(ks[19], (L, N_Q_HEADS), 0.5),
        'w_c_out': nrm(ks[20], (L, N_Q_HEADS * HEAD_DIM, D), (N_Q_HEADS * HEAD_DIM) ** -0.5),
        'w_d_out': nrm(ks[21], (L, D_FOURIER, D), D_FOURIER ** -0.5),
        'w_o': nrm(ks[22], (L, D, D), D ** -0.5),
        'router_w': nrm(ks[23], (L, D, N_EXPERTS), D ** -0.5),
        'exp_w1': nrm(ks[24], (L, N_EXPERTS, D, D_EXPERT), D ** -0.5),
        'exp_w3': nrm(ks[25], (L, N_EXPERTS, D, D_EXPERT), D ** -0.5),
        'exp_w2': nrm(ks[26], (L, N_EXPERTS, D_EXPERT, D), D_EXPERT ** -0.5),
    }


def reference(x, c, ctx, c_ctx, ada_w, ada_b, pre_mix_g, post_mix_g, pre_ffn_g, post_ffn_g,
              w_in, gate_b, conv_a_w, conv_a_b, ln_a_g, ln_a_b, w_a_out, conv_b_w, w_b_out,
              sink, w_c_out, w_d_out, w_o, router_w, exp_w1, exp_w3, exp_w2):
    rope = axial_rope_tables(x.shape[1])
    xs, cs = x, ctx
    for l in range(DEPTH):
        last = l == DEPTH - 1
        mx = jnp.split((jax.nn.silu(c) @ ada_w[l] + ada_b[l])[:, None, :], 6, axis=-1)
        mc = jnp.split((jax.nn.silu(c_ctx) @ ada_w[l] + ada_b[l])[None, None, :], 6, axis=-1)
        hx = modulate(rms_norm(xs, pre_mix_g[l]), mx[0], mx[1])
        hc = modulate(rms_norm(cs, pre_mix_g[l]), mc[0], mc[1])
        yx, yc = mixer_sublayer(hx, hc, not last, rope, w_in[l], gate_b[l], conv_a_w[l], conv_a_b[l],
                                ln_a_g[l], ln_a_b[l], w_a_out[l], conv_b_w[l], w_b_out[l], sink[l],
                                w_c_out[l], w_d_out[l], w_o[l])
        xs = xs + mx[2] * rms_norm(yx, post_mix_g[l])
        hx = modulate(rms_norm(xs, pre_ffn_g[l]), mx[3], mx[4])
        xs = xs + mx[5] * rms_norm(expert_choice_moe(hx, router_w[l], exp_w1[l], exp_w3[l], exp_w2[l]),
                                   post_ffn_g[l])
        if not last:
            cs = cs + mc[2] * rms_norm(yc, post_mix_g[l])
            hc = modulate(rms_norm(cs, pre_ffn_g[l]), mc[3], mc[4])
            cs = cs + mc[5] * rms_norm(expert_choice_moe(hc, router_w[l], exp_w1[l], exp_w3[l], exp_w2[l]),
                                       post_ffn_g[l])
    return xs
```

```python
import functools
import math

import jax
import jax.numpy as jnp
from jax import lax
from jax.experimental import pallas as pl
from jax.experimental.pallas import tpu as pltpu

F32 = jnp.float32
BF16 = jnp.bfloat16
I32 = jnp.int32

D_MODEL = 1024
GRID_W = 64
D_CONV = 512
CONV_K = 31
D_SHORT = 512
SHORT_K = 3
N_Q_HEADS = 8
N_KV_HEADS = 2
HEAD_DIM = 64
BLOCK = 128
D_FOURIER = 512
N_FOURIER_GROUPS = 4
N_BRANCH = 4
N_EXPERTS = 16
CAPACITY_FACTOR = 2
D_EXPERT = 1024
ROPE_BASE = 10000.0
EPS = 1e-6
NEG_INF = -1e30

LANES = 128
HALO = 16
CONV_CHUNK = 64

COL_A = 0
COL_SHORT = 1024
COL_Q = 2560
COL_F = 3072
COL_KV = 3584
COL_G = 4096
N_PROJ = 8192


def _cparams(sem, vmem_mb):
    return pltpu.CompilerParams(dimension_semantics=sem,
                                vmem_limit_bytes=vmem_mb * 1024 * 1024)


def _sigmoid(x):
    return 1.0 / (1.0 + jnp.exp(-x))


def _silu(x):
    return x * _sigmoid(x)


def _rms(x, g):
    return x * lax.rsqrt(jnp.mean(x * x, axis=-1, keepdims=True) + EPS) * g


def _split_bf16(x):
    hi = x.astype(BF16)
    lo = (x - hi.astype(F32)).astype(BF16)
    return hi, lo


def _dot(a, b):
    return jnp.dot(a, b, preferred_element_type=F32)


def _dot3(a, b):
    ah, al = _split_bf16(a)
    bh, bl = _split_bf16(b)
    return _dot(ah, bh) + _dot(ah, bl) + _dot(al, bh)


def _ada_kernel(c_ref, w_ref, b_ref, o_ref):
    c = c_ref[...]
    o_ref[0] = _dot3(_silu(c), w_ref[0]) + b_ref[0]


def _ada(cvec, ada_w, ada_b):
    nl, d, n6 = ada_w.shape
    rows = cvec.shape[0]
    return pl.pallas_call(
        _ada_kernel,
        grid=(nl, n6 // d),
        in_specs=[pl.BlockSpec((rows, d), lambda l, j: (0, 0)),
                  pl.BlockSpec((1, d, d), lambda l, j: (l, 0, j)),
                  pl.BlockSpec((1, 1, d), lambda l, j: (l, 0, j))],
        out_specs=pl.BlockSpec((1, rows, d), lambda l, j: (l, 0, j)),
        out_shape=jax.ShapeDtypeStruct((nl, rows, n6), F32),
        compiler_params=_cparams(("parallel", "parallel"), 40),
        name="ada_mod",
    )(cvec, ada_w, ada_b.reshape(nl, 1, n6))


def _inproj_kernel(x_ref, sh_ref, sc_ref, g_ref, w_ref, o_ref, h_ref):
    @pl.when(pl.program_id(1) == 0)
    def _():
        y = _rms(x_ref[...], g_ref[...])
        h_ref[...] = (y * (1.0 + sc_ref[0]) + sh_ref[0]).astype(BF16)

    o_ref[...] = _dot(h_ref[...], w_ref[...]).astype(BF16)


def _inproj(x2d, shift, scale, gain, w, rows_per_group):
    r, d = x2d.shape
    n = w.shape[1]
    tm = min(1024, rows_per_group)
    tn = 1024
    tiles_per_group = rows_per_group // tm
    mod_spec = pl.BlockSpec((1, 1, d), lambda i, j: (i // tiles_per_group, 0, 0))
    return pl.pallas_call(
        _inproj_kernel,
        grid=(r // tm, n // tn),
        in_specs=[pl.BlockSpec((tm, d), lambda i, j: (i, 0)),
                  mod_spec, mod_spec,
                  pl.BlockSpec((1, d), lambda i, j: (0, 0)),
                  pl.BlockSpec((d, tn), lambda i, j: (0, j))],
        out_specs=pl.BlockSpec((tm, tn), lambda i, j: (i, j)),
        out_shape=jax.ShapeDtypeStruct((r, n), BF16),
        scratch_shapes=[pltpu.VMEM((tm, d), BF16)],
        compiler_params=_cparams(("parallel", "arbitrary"), 48),
        name="inproj",
    )(x2d, shift, scale, gain, w)


def _mm_kernel(a_ref, b_ref, o_ref):
    o_ref[...] = _dot(a_ref[...], b_ref[...]).astype(o_ref.dtype)


def _mm(a, b, *, a_cols=None, out_dtype=BF16, tm=1024):
    k, n = b.shape
    r = a.shape[0]
    cb = 0 if a_cols is None else a_cols // k
    tm = min(tm, r)
    return pl.pallas_call(
        _mm_kernel,
        grid=(r // tm,),
        in_specs=[pl.BlockSpec((tm, k), lambda i: (i, cb)),
                  pl.BlockSpec((k, n), lambda i: (0, 0))],
        out_specs=pl.BlockSpec((tm, n), lambda i: (i, 0)),
        out_shape=jax.ShapeDtypeStruct((r, n), out_dtype),
        compiler_params=_cparams(("parallel",), 40),
        name="matmul",
    )(a, b)


def _fill_window(win_ref, cur, prev, nxt, t):
    n = pl.program_id(1)
    last = pl.num_programs(1) - 1
    win_ref[HALO:HALO + t, :] = cur
    win_ref[0:HALO, :] = jnp.where(n > 0, prev, 0.0)
    win_ref[HALO + t:HALO + t + HALO, :] = jnp.where(n < last, nxt, 0.0)


def _dwconv_chunk(win_ref, t0, w_ref, ktaps):
    off = HALO - ktaps // 2
    span = off + ktaps - 1
    nfull = -(-(span + 1) // 8) * 8
    w = win_ref[pl.ds(t0, CONV_CHUNK + nfull), :]
    acc = None
    for r in range(8):
        steps = [a for a in range(nfull // 8) if 0 <= 8 * a + r - off < ktaps]
        if not steps:
            continue
        wr = w[r:r + CONV_CHUNK + 8 * (nfull // 8 - 1)]
        for a in steps:
            j = 8 * a + r - off
            term = wr[8 * a:8 * a + CONV_CHUNK] * w_ref[j:j + 1, :]
            acc = term if acc is None else acc + term
    return acc


def _conformer_kernel(cur_ref, prev_ref, next_ref, cw_ref, cb_ref, lg_ref, lb_ref,
                      o_ref, win_ref, *, t):
    dc = o_ref.shape[-1]

    def glu(ref):
        blk = ref[...].astype(F32)
        return blk[:, :dc] * _sigmoid(blk[:, dc:])

    _fill_window(win_ref, glu(cur_ref), glu(prev_ref), glu(next_ref), t)

    def chunk(i, carry):
        t0 = pl.multiple_of(i * CONV_CHUNK, CONV_CHUNK)
        h = _dwconv_chunk(win_ref, t0, cw_ref, CONV_K) + cb_ref[...]
        mu = jnp.mean(h, axis=-1, keepdims=True)
        hc = h - mu
        y = hc * lax.rsqrt(jnp.mean(hc * hc, axis=-1, keepdims=True) + EPS)
        y = y * lg_ref[...] + lb_ref[...]
        o_ref[pl.ds(t0, CONV_CHUNK), :] = _silu(y).astype(BF16)
        return carry

    lax.fori_loop(0, t // CONV_CHUNK, chunk, 0)


def _halo_specs(width, col_block, s, t):
    nblk = t // HALO
    per_sample = s // HALO

    def prev_map(b, n):
        return (jnp.maximum(b * per_sample + n * nblk - 1, 0), col_block)

    def next_map(b, n):
        return (jnp.minimum(b * per_sample + (n + 1) * nblk, (b + 1) * per_sample - 1), col_block)

    return pl.BlockSpec((HALO, width), prev_map), pl.BlockSpec((HALO, width), next_map)


def _conformer(p, conv_w, conv_b, ln_g, ln_b, batch, s):
    t = min(512, s)
    nt = s // t
    prev_spec, next_spec = _halo_specs(2 * D_CONV, COL_A // (2 * D_CONV), s, t)
    vec = pl.BlockSpec((1, D_CONV), lambda b, n: (0, 0))
    return pl.pallas_call(
        functools.partial(_conformer_kernel, t=t),
        grid=(batch, nt),
        in_specs=[pl.BlockSpec((t, 2 * D_CONV), lambda b, n: (b * nt + n, COL_A // (2 * D_CONV))),
                  prev_spec, next_spec,
                  pl.BlockSpec((CONV_K, D_CONV), lambda b, n: (0, 0)),
                  vec, vec, vec],
        out_specs=pl.BlockSpec((t, D_CONV), lambda b, n: (b * nt + n, 0)),
        out_shape=jax.ShapeDtypeStruct((batch * s, D_CONV), BF16),
        scratch_shapes=[pltpu.VMEM((t + 2 * HALO + 8, D_CONV), F32)],
        compiler_params=_cparams(("parallel", "parallel"), 40),
        name="conformer_conv",
    )(p, p, p, conv_w, conv_b.reshape(1, -1), ln_g.reshape(1, -1), ln_b.reshape(1, -1))


def _short_kernel(bg_ref, cg_ref, hv_ref, cgp_ref, hvp_ref, cgn_ref, hvn_ref, w_ref,
                  o_ref, win_ref, *, t):
    def prod(a_ref, b_ref):
        return a_ref[...].astype(F32) * b_ref[...].astype(F32)

    _fill_window(win_ref, prod(cg_ref, hv_ref), prod(cgp_ref, hvp_ref), prod(cgn_ref, hvn_ref), t)

    def chunk(i, carry):
        t0 = pl.multiple_of(i * CONV_CHUNK, CONV_CHUNK)
        h = _dwconv_chunk(win_ref, t0, w_ref, SHORT_K)
        bg = bg_ref[pl.ds(t0, CONV_CHUNK), :].astype(F32)
        o_ref[pl.ds(t0, CONV_CHUNK), :] = (bg * h).astype(BF16)
        return carry

    lax.fori_loop(0, t // CONV_CHUNK, chunk, 0)


def _short_conv(p, conv_w, batch, s):
    t = min(512, s)
    nt = s // t
    cb = COL_SHORT // D_SHORT
    cgp, cgn = _halo_specs(D_SHORT, cb + 1, s, t)
    hvp, hvn = _halo_specs(D_SHORT, cb + 2, s, t)

    def cur(k):
        return pl.BlockSpec((t, D_SHORT), lambda b, n: (b * nt + n, cb + k))

    return pl.pallas_call(
        functools.partial(_short_kernel, t=t),
        grid=(batch, nt),
        in_specs=[cur(0), cur(1), cur(2), cgp, hvp, cgn, hvn,
                  pl.BlockSpec((SHORT_K, D_SHORT), lambda b, n: (0, 0))],
        out_specs=pl.BlockSpec((t, D_SHORT), lambda b, n: (b * nt + n, 0)),
        out_shape=jax.ShapeDtypeStruct((batch * s, D_SHORT), BF16),
        scratch_shapes=[pltpu.VMEM((t + 2 * HALO + 8, D_SHORT), F32)],
        compiler_params=_cparams(("parallel", "parallel"), 40),
        name="short_conv",
    )(p, p, p, p, p, p, p, conv_w)


def _rope(x, cos, sin):
    w = x.shape[1]
    lane = lax.broadcasted_iota(I32, x.shape, 1)
    swapped = jnp.where((lane & 31) < 16, pltpu.roll(x, w - 16, 1), pltpu.roll(x, 16, 1))
    return x * cos + swapped * sin


def _stack_heads(q, g):
    grp = N_Q_HEADS // N_KV_HEADS
    return jnp.concatenate(
        [q[:, (grp * g + i) * HEAD_DIM:(grp * g + i + 1) * HEAD_DIM] for i in range(grp)], axis=0)


def _softmax_pv(s, sink_col, vh):
    m = jnp.maximum(jnp.max(s, axis=1, keepdims=True), sink_col)
    e = jnp.exp(s - m)
    den = jnp.sum(e, axis=1, keepdims=True) + jnp.exp(sink_col - m)
    return _dot(e.astype(BF16), vh) / den


def _unstack_heads(outs, nq):
    grp = N_Q_HEADS // N_KV_HEADS
    pieces = [o[i * nq:(i + 1) * nq] for o in outs for i in range(grp)]
    return jnp.concatenate(pieces, axis=1)


def _attn_kernel(q_ref, kv_ref, ckv_ref, cos_ref, sin_ref, bias_ref, sink_ref, o_ref, *, nb):
    n = pl.program_id(1)
    q0 = pl.multiple_of(n * BLOCK, BLOCK)
    sp = pl.multiple_of(jnp.maximum(n - 1, 0) * BLOCK, BLOCK)
    sn = pl.multiple_of(jnp.minimum(n + 1, nb - 1) * BLOCK, BLOCK)
    kvw = N_KV_HEADS * HEAD_DIM

    cq = cos_ref[pl.ds(q0, BLOCK), :]
    sq = sin_ref[pl.ds(q0, BLOCK), :]
    rep = N_Q_HEADS * HEAD_DIM // LANES
    q = _rope(q_ref[...].astype(F32), jnp.concatenate([cq] * rep, axis=1),
              jnp.concatenate([sq] * rep, axis=1)) * (HEAD_DIM ** -0.5)

    def kblock(start):
        kvb = kv_ref[pl.ds(start, BLOCK), :].astype(F32)
        k = _rope(kvb[:, :kvw], cos_ref[pl.ds(start, BLOCK), :], sin_ref[pl.ds(start, BLOCK), :])
        return k, kvb[:, kvw:]

    kp, vp = kblock(sp)
    kc, vc = kblock(q0)
    kn, vn = kblock(sn)
    ckv = ckv_ref[...].astype(F32)
    k_all = jnp.concatenate([kp, kc, kn, ckv[:, :kvw]], axis=0)
    v_all = jnp.concatenate([vp, vc, vn, ckv[:, kvw:]], axis=0)

    col = lax.broadcasted_iota(I32, (1, bias_ref.shape[1]), 1)
    edge = (jnp.where((col < BLOCK) & (n == 0), NEG_INF, 0.0)
            + jnp.where((col >= 2 * BLOCK) & (col < 3 * BLOCK) & (n == nb - 1), NEG_INF, 0.0))
    bias = bias_ref[...] + edge

    outs = []
    for g in range(N_KV_HEADS):
        qs = _stack_heads(q, g).astype(BF16)
        kh = k_all[:, g * HEAD_DIM:(g + 1) * HEAD_DIM].astype(BF16)
        vh = v_all[:, g * HEAD_DIM:(g + 1) * HEAD_DIM].astype(BF16)
        s = lax.dot_general(qs, kh, (((1,), (1,)), ((), ())), preferred_element_type=F32) + bias
        outs.append(_softmax_pv(s, sink_ref[g][:, :1], vh))
    o_ref[...] = _unstack_heads(outs, BLOCK).astype(BF16)


def _band_bias(n_ctx):
    grp = N_Q_HEADS // N_KV_HEADS
    qi = jnp.arange(BLOCK)[:, None]
    kj = jnp.arange(3 * BLOCK)[None, :]
    valid = jnp.abs(kj - BLOCK - qi) <= BLOCK
    band = jnp.where(valid, 0.0, NEG_INF).astype(F32)
    bias = jnp.concatenate([band, jnp.zeros((BLOCK, n_ctx), F32)], axis=1)
    return jnp.tile(bias, (grp, 1))


def _sink_cols(sink, nq):
    grp = N_Q_HEADS // N_KV_HEADS
    col = jnp.repeat(sink.astype(F32).reshape(N_KV_HEADS, grp), nq, axis=1)
    return jnp.broadcast_to(col[:, :, None], (N_KV_HEADS, grp * nq, LANES))


def _latent_attention(p, pc, cos_t, sin_t, sink, batch, s, n_ctx):
    nb = s // BLOCK
    grp = N_Q_HEADS // N_KV_HEADS
    qw = N_Q_HEADS * HEAD_DIM
    kvw2 = 2 * N_KV_HEADS * HEAD_DIM
    bias = _band_bias(n_ctx)
    tab = pl.BlockSpec((s, LANES), lambda b, n: (0, 0))
    return pl.pallas_call(
        functools.partial(_attn_kernel, nb=nb),
        grid=(batch, nb),
        in_specs=[pl.BlockSpec((BLOCK, qw), lambda b, n: (b * nb + n, COL_Q // qw)),
                  pl.BlockSpec((s, kvw2), lambda b, n: (b, COL_KV // kvw2)),
                  pl.BlockSpec((n_ctx, kvw2), lambda b, n: (b, COL_KV // kvw2)),
                  tab, tab,
                  pl.BlockSpec(bias.shape, lambda b, n: (0, 0)),
                  pl.BlockSpec((N_KV_HEADS, grp * BLOCK, LANES), lambda b, n: (0, 0, 0))],
        out_specs=pl.BlockSpec((BLOCK, qw), lambda b, n: (b * nb + n, 0)),
        out_shape=jax.ShapeDtypeStruct((batch * s, qw), BF16),
        compiler_params=_cparams(("parallel", "parallel"), 40),
        name="latent_attention",
    )(p, p, pc, cos_t, sin_t, bias, _sink_cols(sink, BLOCK))


def _cattn_kernel(q_ref, kv_ref, sink_ref, o_ref):
    nq = q_ref.shape[0]
    kvw = N_KV_HEADS * HEAD_DIM
    q = q_ref[...].astype(F32) * (HEAD_DIM ** -0.5)
    kv = kv_ref[...].astype(F32)
    outs = []
    for g in range(N_KV_HEADS):
        qs = _stack_heads(q, g).astype(BF16)
        kh = kv[:, g * HEAD_DIM:(g + 1) * HEAD_DIM].astype(BF16)
        vh = kv[:, kvw + g * HEAD_DIM:kvw + (g + 1) * HEAD_DIM].astype(BF16)
        s = lax.dot_general(qs, kh, (((1,), (1,)), ((), ())), preferred_element_type=F32)
        outs.append(_softmax_pv(s, sink_ref[g][:, :1], vh))
    o_ref[...] = _unstack_heads(outs, nq).astype(BF16)


def _context_attention(pc, sink, batch, n_ctx):
    grp = N_Q_HEADS // N_KV_HEADS
    qw = N_Q_HEADS * HEAD_DIM
    kvw2 = 2 * N_KV_HEADS * HEAD_DIM
    return pl.pallas_call(
        _cattn_kernel,
        grid=(batch,),
        in_specs=[pl.BlockSpec((n_ctx, qw), lambda b: (b, COL_Q // qw)),
                  pl.BlockSpec((n_ctx, kvw2), lambda b: (b, COL_KV // kvw2)),
                  pl.BlockSpec((N_KV_HEADS, grp * n_ctx, LANES), lambda b: (0, 0, 0))],
        out_specs=pl.BlockSpec((n_ctx, qw), lambda b: (b, 0)),
        out_shape=jax.ShapeDtypeStruct((batch * n_ctx, qw), BF16),
        compiler_params=_cparams(("parallel",), 40),
        name="context_attention",
    )(pc, pc, _sink_cols(sink, n_ctx))


def _dft_kernel(c_ref, s_ref, v_ref, o_ref):
    half = o_ref.shape[-1]
    o_ref[...] = (_dot(c_ref[...], v_ref[:, :half]) - _dot(s_ref[...], v_ref[:, half:])).astype(BF16)


def _position_dft(v, cos_n, sin_n, batch, s):
    tm = min(512, s)
    nt = s // tm
    return pl.pallas_call(
        _dft_kernel,
        grid=(nt, batch),
        in_specs=[pl.BlockSpec((tm, s), lambda r, b: (r, 0)),
                  pl.BlockSpec((tm, s), lambda r, b: (r, 0)),
                  pl.BlockSpec((s, 2 * D_FOURIER), lambda r, b: (b, 0))],
        out_specs=pl.BlockSpec((tm, D_FOURIER), lambda r, b: (b * nt + r, 0)),
        out_shape=jax.ShapeDtypeStruct((batch * s, D_FOURIER), BF16),
        compiler_params=_cparams(("parallel", "parallel"), 48),
        name="position_dft",
    )(cos_n, sin_n, v)


def _dft_tables(n):
    k = jnp.arange(n, dtype=I32)
    ang = ((k[:, None] * k[None, :]) % n).astype(F32) * (2.0 * math.pi / n)
    scale = n ** -0.5
    return jnp.cos(ang) * scale, jnp.sin(ang) * scale


def _channel_dft_matrix():
    cg = D_FOURIER // N_FOURIER_GROUPS
    c, s = _dft_tables(cg)
    eye = jnp.eye(N_FOURIER_GROUPS, dtype=F32)
    return jnp.concatenate([jnp.kron(eye, c), jnp.kron(eye, s)], axis=1).astype(BF16)


def _merge_kernel(ba_ref, bb_ref, bc_ref, bd_ref, gl_ref, gb_ref, wa_ref, wb_ref, wc_ref, wd_ref,
                  wo_ref, xs_ref, m2_ref, m3_ref, m4_ref, pg_ref, fg_ref, rw_ref,
                  xo_ref, hp_ref, aff_ref):
    d = xs_ref.shape[-1]
    y = None
    for i, (b_ref, w_ref) in enumerate(((ba_ref, wa_ref), (bb_ref, wb_ref),
                                        (bc_ref, wc_ref), (bd_ref, wd_ref))):
        gate = _sigmoid(gl_ref[:, i * d:(i + 1) * d].astype(F32) + gb_ref[:, i * d:(i + 1) * d])
        term = gate * _dot(b_ref[...], w_ref[...])
        y = term if y is None else y + term
    z = _dot(y.astype(BF16), wo_ref[...])
    xs = xs_ref[...] + m2_ref[0] * _rms(z, pg_ref[...])
    xo_ref[...] = xs
    h = _rms(xs, fg_ref[...]) * (1.0 + m4_ref[0]) + m3_ref[0]
    hb = h.astype(BF16).astype(F32)
    lo = lax.shift_right_logical(lax.bitcast_convert_type(hb[:, :d // 2], I32), 16)
    hi = lax.bitcast_convert_type(hb[:, d // 2:], I32) & (-65536)
    hp_ref[...] = hi | lo
    logits = _dot3(h, rw_ref[...])
    lane = lax.broadcasted_iota(I32, logits.shape, 1)
    logits = jnp.where(lane < N_EXPERTS, logits, NEG_INF)
    e = jnp.exp(logits - jnp.max(logits, axis=1, keepdims=True))
    aff_ref[...] = e / jnp.sum(e, axis=1, keepdims=True)


def _merge(ba, bb, bc, bd, p, gate_b, wa, wb, wc, wd, wo, xs, m2, m3, m4, post_g, ffn_g, rw,
           rows_per_group):
    r, d = xs.shape
    tm = min(512, rows_per_group)
    tiles_per_group = rows_per_group // tm
    gw = N_BRANCH * d

    def rows(width):
        return pl.BlockSpec((tm, width), lambda i: (i, 0))

    def const(shape):
        return pl.BlockSpec(shape, lambda i: (0,) * len(shape))

    mod = pl.BlockSpec((1, 1, d), lambda i: (i // tiles_per_group, 0, 0))
    half = d // 2
    return pl.pallas_call(
        _merge_kernel,
        grid=(r // tm,),
        in_specs=[rows(half), rows(half), rows(half), rows(half),
                  pl.BlockSpec((tm, gw), lambda i: (i, COL_G // gw)),
                  const((1, gw)),
                  const((half, d)), const((half, d)), const((half, d)), const((half, d)),
                  const((d, d)),
                  rows(d), mod, mod, mod, const((1, d)), const((1, d)), const((d, LANES))],
        out_specs=[rows(d), rows(half), rows(LANES)],
        out_shape=[jax.ShapeDtypeStruct((r, d), F32),
                   jax.ShapeDtypeStruct((r, half), I32),
                   jax.ShapeDtypeStruct((r, LANES), F32)],
        compiler_params=_cparams(("parallel",), 48),
        name="merge_router",
    )(ba, bb, bc, bd, p, gate_b.reshape(1, gw), wa, wb, wc, wd, wo, xs, m2, m3, m4,
      post_g.reshape(1, d), ffn_g.reshape(1, d), rw)


def _route_kernel(aff_ref, tri_ref, tl_ref, idx_ref, *, cap):
    s = aff_ref.shape[0]
    nslot = idx_ref.shape[-1]
    bits = lax.bitcast_convert_type(aff_ref[...], I32)

    def count(mask):
        return jnp.sum(jnp.where(mask, 1.0, 0.0), axis=0, keepdims=True)

    def search(i, thr):
        cand = thr | lax.shift_left(jnp.int32(1), 30 - i)
        return jnp.where(count(bits >= cand) >= cap, cand, thr)

    thr = lax.fori_loop(0, 31, search, jnp.zeros((1, LANES), I32))
    gt = bits > thr
    eq = bits == thr
    need = cap - count(gt)

    tri = tri_ref[...]

    def cumsum_excl(m):
        off = jnp.zeros((1, LANES), F32)
        outs = []
        for c in range(s // LANES):
            mc = m[c * LANES:(c + 1) * LANES]
            cs = _dot(tri, mc.astype(BF16))
            outs.append(cs - mc + off)
            off = off + cs[LANES - 1:LANES, :]
        return jnp.concatenate(outs, axis=0)

    eq_f = jnp.where(eq, 1.0, 0.0)
    sel = gt | (eq & (cumsum_excl(eq_f) < need))
    sel_f = jnp.where(sel, 1.0, 0.0)
    pos = jnp.where(sel, cumsum_excl(sel_f), -1.0)

    slot = lax.broadcasted_iota(I32, (s, nslot), 1).astype(F32)
    tl = tl_ref[...]
    for e in range(N_EXPERTS):
        onehot = jnp.where(pos[:, e:e + 1] == slot, 1.0, 0.0).astype(BF16)
        res = _dot(tl, onehot)
        idx_ref[0, e:e + 1, :] = (res[0:1] * 64.0 + res[1:2] + 0.5).astype(I32)


def _route(aff, batch, s, cap):
    nslot = max(cap, LANES)
    tri = (jnp.arange(LANES)[:, None] >= jnp.arange(LANES)[None, :]).astype(BF16)
    t = jnp.arange(s)
    tl = jnp.zeros((8, s), F32).at[0].set(t // 64).at[1].set(t % 64).astype(BF16)
    idx = pl.pallas_call(
        functools.partial(_route_kernel, cap=cap),
        grid=(batch,),
        in_specs=[pl.BlockSpec((s, LANES), lambda b: (b, 0)),
                  pl.BlockSpec((LANES, LANES), lambda b: (0, 0)),
                  pl.BlockSpec((8, s), lambda b: (0, 0))],
        out_specs=pl.BlockSpec((1, N_EXPERTS, nslot), lambda b: (b, 0, 0)),
        out_shape=jax.ShapeDtypeStruct((batch, N_EXPERTS, nslot), I32),
        compiler_params=_cparams(("parallel",), 48),
        name="route_topk",
    )(aff, tri, tl)
    return idx[:, :, :cap].reshape(-1)


def _gather_kernel(idx_ref, h_ref, aff_ref, xg_ref, val_ref, g_ref, ga_ref, *, cap):
    b = pl.program_id(0)
    e = pl.program_id(1)
    base = (b * N_EXPERTS + e) * cap

    def body(j, carry):
        t = idx_ref[base + j]
        g_ref[pl.ds(j, 1), :] = h_ref[pl.ds(t, 1), :]
        ga_ref[pl.ds(j, 1), :] = aff_ref[pl.ds(t, 1), :]
        return carry

    lax.fori_loop(0, cap, body, 0, unroll=8)
    u = g_ref[...]
    half = u.shape[1]
    xg_ref[:, :half] = lax.bitcast_convert_type(lax.shift_left(u, 16), F32).astype(BF16)
    xg_ref[:, half:] = lax.bitcast_convert_type(u & (-65536), F32).astype(BF16)
    lane = lax.broadcasted_iota(I32, ga_ref.shape, 1)
    v = jnp.sum(jnp.where(lane == e, ga_ref[...], 0.0), axis=1, keepdims=True)
    val_ref[...] = jnp.broadcast_to(v, val_ref.shape)


def _gather(idx, hp, aff, batch, s, cap):
    half = hp.shape[1]
    grid_spec = pltpu.PrefetchScalarGridSpec(
        num_scalar_prefetch=1,
        grid=(batch, N_EXPERTS),
        in_specs=[pl.BlockSpec((s, half), lambda b, e, idx: (b, 0)),
                  pl.BlockSpec((s, LANES), lambda b, e, idx: (b, 0))],
        out_specs=[pl.BlockSpec((cap, 2 * half), lambda b, e, idx: (e * batch + b, 0)),
                   pl.BlockSpec((cap, LANES), lambda b, e, idx: (e * batch + b, 0))],
        scratch_shapes=[pltpu.VMEM((cap, half), I32), pltpu.VMEM((cap, LANES), F32)])
    rows = N_EXPERTS * batch * cap
    return pl.pallas_call(
        functools.partial(_gather_kernel, cap=cap),
        grid_spec=grid_spec,
        out_shape=[jax.ShapeDtypeStruct((rows, 2 * half), BF16),
                   jax.ShapeDtypeStruct((rows, LANES), F32)],
        compiler_params=_cparams(("arbitrary", "arbitrary"), 40),
        name="moe_gather",
    )(idx, hp, aff)


def _ffn_kernel(x_ref, v_ref, w1_ref, w3_ref, w2_ref, y_ref, w1b, w3b, w2b):
    @pl.when(pl.program_id(1) == 0)
    def _():
        w1b[...] = w1_ref[0, 0].astype(BF16)
        w3b[...] = w3_ref[0, 0].astype(BF16)
        w2b[...] = w2_ref[0, 0].astype(BF16)

    x = x_ref[...]
    hid = _silu(_dot(x, w1b[...])) * _dot(x, w3b[...])
    y = _dot(hid.astype(BF16), w2b[...])
    y_ref[...] = y * jnp.concatenate([v_ref[...]] * (y.shape[1] // LANES), axis=1)


def _ffn(xg, vals, w1, w3, w2, layer, rows_per_expert):
    rows, d = xg.shape
    f = w1.shape[-1]
    tm = min(512, rows_per_expert)
    nt = rows_per_expert // tm

    def wspec(a, c):
        return pl.BlockSpec((1, 1, a, c), lambda e, m: (layer, e, 0, 0))

    return pl.pallas_call(
        _ffn_kernel,
        grid=(N_EXPERTS, nt),
        in_specs=[pl.BlockSpec((tm, d), lambda e, m: (e * nt + m, 0)),
                  pl.BlockSpec((tm, LANES), lambda e, m: (e * nt + m, 0)),
                  wspec(d, f), wspec(d, f), wspec(f, d)],
        out_specs=pl.BlockSpec((tm, d), lambda e, m: (e * nt + m, 0)),
        out_shape=jax.ShapeDtypeStruct((rows, d), F32),
        scratch_shapes=[pltpu.VMEM((d, f), BF16), pltpu.VMEM((d, f), BF16), pltpu.VMEM((f, d), BF16)],
        compiler_params=_cparams(("parallel", "arbitrary"), 56),
        name="expert_ffn",
    )(xg, vals, w1, w3, w2)


def _combine_kernel(idx_ref, y_ref, xs_ref, m5_ref, g_ref, o_ref, acc_ref, *, cap, tf):
    b = pl.program_id(0)
    step = pl.program_id(1)

    @pl.when(step == 0)
    def _():
        acc_ref[...] = jnp.zeros_like(acc_ref)

    @pl.when(step < N_EXPERTS)
    def _():
        base = (b * N_EXPERTS + step) * cap
        group = 8

        def body(jg, carry):
            j0 = pl.multiple_of(jg * group, group)
            toks = [idx_ref[base + j0 + k] for k in range(group)]
            rows = [acc_ref[pl.ds(tk, 1), :] for tk in toks]
            for k in range(group):
                acc_ref[pl.ds(toks[k], 1), :] = rows[k] + y_ref[pl.ds(j0 + k, 1), :]
            return carry

        lax.fori_loop(0, cap // group, body, 0)

    @pl.when(step >= N_EXPERTS)
    def _():
        r0 = pl.multiple_of((step - N_EXPERTS) * tf, tf)
        o_ref[...] = xs_ref[...] + m5_ref[0] * _rms(acc_ref[pl.ds(r0, tf), :], g_ref[...])


def _combine(idx, y, xs, m5, post_g, batch, s, cap, shared_mod):
    d = xs.shape[1]
    tf = min(512, s)
    nfin = s // tf

    def chunk_map(b, st, idx):
        return (b * nfin + jnp.maximum(st - N_EXPERTS, 0), 0)

    grid_spec = pltpu.PrefetchScalarGridSpec(
        num_scalar_prefetch=1,
        grid=(batch, N_EXPERTS + nfin),
        in_specs=[pl.BlockSpec((cap, d), lambda b, st, idx: (jnp.minimum(st, N_EXPERTS - 1) * batch + b, 0)),
                  pl.BlockSpec((tf, d), chunk_map),
                  pl.BlockSpec((1, 1, d), lambda b, st, idx: (0 if shared_mod else b, 0, 0)),
                  pl.BlockSpec((1, d), lambda b, st, idx: (0, 0))],
        out_specs=pl.BlockSpec((tf, d), chunk_map),
        scratch_shapes=[pltpu.VMEM((s, d), F32)])
    return pl.pallas_call(
        functools.partial(_combine_kernel, cap=cap, tf=tf),
        grid_spec=grid_spec,
        out_shape=jax.ShapeDtypeStruct(xs.shape, F32),
        compiler_params=_cparams(("arbitrary", "arbitrary"), 48),
        name="moe_combine",
    )(idx, y, xs, m5, post_g.reshape(1, d))


def _prep_w_in(w):
    d = w.shape[0]
    kv = 2 * N_KV_HEADS * HEAD_DIM
    o_q = 2 * D_CONV + 3 * D_SHORT
    o_k = o_q + N_Q_HEADS * HEAD_DIM
    o_f = o_k + kv
    o_g = o_f + D_FOURIER
    parts = [w[:, :o_k], w[:, o_f:o_g], w[:, o_k:o_f],
             jnp.zeros((d, COL_G - COL_KV - kv), w.dtype), w[:, o_g:]]
    return jnp.concatenate(parts, axis=1).astype(BF16)


def _rope_tables(s):
    t = jnp.arange(s)
    row = (t // GRID_W).astype(F32)
    col = (t % GRID_W).astype(F32)
    nf = HEAD_DIM // 4
    inv = ROPE_BASE ** (-jnp.arange(nf, dtype=F32) / nf)
    ar = row[:, None] * inv
    ac = col[:, None] * inv
    cos = jnp.concatenate([jnp.cos(ar), jnp.cos(ar), jnp.cos(ac), jnp.cos(ac)], axis=1)
    sin = jnp.concatenate([-jnp.sin(ar), jnp.sin(ar), -jnp.sin(ac), jnp.sin(ac)], axis=1)
    rep = LANES // HEAD_DIM
    return jnp.tile(cos, (1, rep)), jnp.tile(sin, (1, rep))


def _moe(hp, aff, xs, m5, post_g, w1, w3, w2, layer, batch, s, shared_mod):
    cap = CAPACITY_FACTOR * s // N_EXPERTS
    idx = _route(aff, batch, s, cap)
    xg, vals = _gather(idx, hp, aff, batch, s, cap)
    y = _ffn(xg, vals, w1, w3, w2, layer, batch * cap)
    return _combine(idx, y, xs, m5, post_g, batch, s, cap, shared_mod)


def kernel(x, c, ctx, c_ctx, ada_w, ada_b, pre_mix_g, post_mix_g, pre_ffn_g, post_ffn_g, w_in, gate_b, conv_a_w, conv_a_b, ln_a_g, ln_a_b, w_a_out, conv_b_w, w_b_out, sink, w_c_out, w_d_out, w_o, router_w, exp_w1, exp_w3, exp_w2):
    batch, s, d = x.shape
    n_ctx = ctx.shape[1]
    depth = ada_w.shape[0]

    cvec = jnp.zeros((16, d), F32).at[:batch].set(c).at[batch].set(c_ctx)
    mod = _ada(cvec, ada_w, ada_b)
    cos_t, sin_t = _rope_tables(s)
    bd = _channel_dft_matrix()
    dft_x = tuple(t.astype(BF16) for t in _dft_tables(s))
    dft_c = tuple(t.astype(BF16) for t in _dft_tables(n_ctx))

    xs = x.reshape(batch * s, d)
    cs = ctx.reshape(batch * n_ctx, d)
    for l in range(depth):
        last = l == depth - 1
        mx = [mod[l, :batch, k * d:(k + 1) * d].reshape(batch, 1, d) for k in range(6)]
        mc = [mod[l, batch:batch + 1, k * d:(k + 1) * d].reshape(1, 1, d) for k in range(6)]
        g_pre = pre_mix_g[l].reshape(1, d)
        w = _prep_w_in(w_in[l])
        wa, wb, wc, wd, wo = (t[l].astype(BF16) for t in (w_a_out, w_b_out, w_c_out, w_d_out, w_o))
        rw = jnp.zeros((d, LANES), F32).at[:, :N_EXPERTS].set(router_w[l])

        p = _inproj(xs, mx[0], mx[1], g_pre, w, s)
        pc = _inproj(cs, mc[0], mc[1], g_pre, w, batch * n_ctx)

        def mixer(pp, att, seq, tables, xres, m, rows_per_group):
            ba = _conformer(pp, conv_a_w[l], conv_a_b[l], ln_a_g[l], ln_a_b[l], batch, seq)
            bb = _short_conv(pp, conv_b_w[l], batch, seq)
            v = _mm(pp, bd, a_cols=COL_F)
            bf = _position_dft(v, tables[0], tables[1], batch, seq)
            return _merge(ba, bb, att, bf, pp, gate_b[l], wa, wb, wc, wd, wo, xres,
                          m[2], m[3], m[4], post_mix_g[l], pre_ffn_g[l], rw, rows_per_group)

        att_x = _latent_attention(p, pc, cos_t, sin_t, sink[l], batch, s, n_ctx)
        xs, hp, aff = mixer(p, att_x, s, dft_x, xs, mx, s)
        xs = _moe(hp, aff, xs, mx[5], post_ffn_g[l], exp_w1, exp_w3, exp_w2, l, batch, s, False)
        if not last:
            att_c = _context_attention(pc, sink[l], batch, n_ctx)
            cs, hpc, affc = mixer(pc, att_c, n_ctx, dft_c, cs, mc, batch * n_ctx)
            cs = _moe(hpc, affc, cs, mc[5], post_ffn_g[l], exp_w1, exp_w3, exp_w2, l, batch, n_ctx, True)
    return xs.reshape(batch, s, d)
```

```python
import functools
import math

import jax
import jax.numpy as jnp
from jax import lax
from jax.experimental import pallas as pl
from jax.experimental.pallas import tpu as pltpu

F32 = jnp.float32
BF16 = jnp.bfloat16
I32 = jnp.int32

D_MODEL = 1024
GRID_W = 64
D_CONV = 512
CONV_K = 31
D_SHORT = 512
SHORT_K = 3
N_Q_HEADS = 8
N_KV_HEADS = 2
HEAD_DIM = 64
BLOCK = 128
D_FOURIER = 512
N_FOURIER_GROUPS = 4
N_BRANCH = 4
N_EXPERTS = 16
CAPACITY_FACTOR = 2
D_EXPERT = 1024
ROPE_BASE = 10000.0
EPS = 1e-6
NEG_INF = -1e30
LOG2E = 1.4426950408889634
MIN_NORMAL_BITS = 0x00800000

LANES = 128
HALO = 16
CONV_CHUNK = 64

COL_A = 0
COL_SHORT = 1024
COL_Q = 2560
COL_F = 3072
COL_KV = 3584
COL_G = 4096
N_PROJ = 8192


def _cparams(sem, vmem_mb):
    return pltpu.CompilerParams(dimension_semantics=sem,
                                vmem_limit_bytes=vmem_mb * 1024 * 1024)


def _sigmoid(x):
    return 0.5 * jnp.tanh(0.5 * x) + 0.5


def _tanh_gate(half_x):
    return jnp.tanh(half_x) + 1.0


def _silu(x):
    return x * _sigmoid(x)


def _rms(x, g):
    return x * lax.rsqrt(jnp.mean(x * x, axis=-1, keepdims=True) + EPS) * g


def _split_bf16(x):
    hi = x.astype(BF16)
    lo = (x - hi.astype(F32)).astype(BF16)
    return hi, lo


def _dot(a, b):
    return jnp.dot(a, b, preferred_element_type=F32)


def _dot3(a, b):
    ah, al = _split_bf16(a)
    bh, bl = _split_bf16(b)
    return _dot(ah, bh) + _dot(ah, bl) + _dot(al, bh)


def _ada_kernel(c_ref, w_ref, b_ref, o_ref):
    c = c_ref[...]
    o_ref[0] = _dot3(_silu(c), w_ref[0]) + b_ref[0]


def _ada(cvec, ada_w, ada_b):
    nl, d, n6 = ada_w.shape
    rows = cvec.shape[0]
    return pl.pallas_call(
        _ada_kernel,
        grid=(nl, n6 // d),
        in_specs=[pl.BlockSpec((rows, d), lambda l, j: (0, 0)),
                  pl.BlockSpec((1, d, d), lambda l, j: (l, 0, j)),
                  pl.BlockSpec((1, 1, d), lambda l, j: (l, 0, j))],
        out_specs=pl.BlockSpec((1, rows, d), lambda l, j: (l, 0, j)),
        out_shape=jax.ShapeDtypeStruct((nl, rows, n6), F32),
        compiler_params=_cparams(("parallel", "parallel"), 40),
        name="ada_mod",
    )(cvec, ada_w, ada_b.reshape(nl, 1, n6))


def _inproj_kernel(x_ref, sh_ref, sc_ref, g_ref, w_ref, o_ref, h_ref):
    @pl.when(pl.program_id(1) == 0)
    def _():
        y = _rms(x_ref[...], g_ref[...])
        h_ref[...] = (y * (1.0 + sc_ref[0]) + sh_ref[0]).astype(BF16)

    o_ref[...] = _dot(h_ref[...], w_ref[...]).astype(BF16)


def _inproj(x2d, shift, scale, gain, w, rows_per_group):
    r, d = x2d.shape
    n = w.shape[1]
    tm = min(1024, rows_per_group)
    tn = 1024
    tiles_per_group = rows_per_group // tm
    mod_spec = pl.BlockSpec((1, 1, d), lambda i, j: (i // tiles_per_group, 0, 0))
    return pl.pallas_call(
        _inproj_kernel,
        grid=(r // tm, n // tn),
        in_specs=[pl.BlockSpec((tm, d), lambda i, j: (i, 0)),
                  mod_spec, mod_spec,
                  pl.BlockSpec((1, d), lambda i, j: (0, 0)),
                  pl.BlockSpec((d, tn), lambda i, j: (0, j))],
        out_specs=pl.BlockSpec((tm, tn), lambda i, j: (i, j)),
        out_shape=jax.ShapeDtypeStruct((r, n), BF16),
        scratch_shapes=[pltpu.VMEM((tm, d), BF16)],
        compiler_params=_cparams(("parallel", "arbitrary"), 48),
        name="inproj",
    )(x2d, shift, scale, gain, w)


def _mm_kernel(a_ref, b_ref, o_ref):
    o_ref[...] = _dot(a_ref[...], b_ref[...]).astype(o_ref.dtype)


def _mm(a, b, *, a_cols=None, out_dtype=BF16, tm=1024):
    k, n = b.shape
    r = a.shape[0]
    cb = 0 if a_cols is None else a_cols // k
    tm = min(tm, r)
    return pl.pallas_call(
        _mm_kernel,
        grid=(r // tm,),
        in_specs=[pl.BlockSpec((tm, k), lambda i: (i, cb)),
                  pl.BlockSpec((k, n), lambda i: (0, 0))],
        out_specs=pl.BlockSpec((tm, n), lambda i: (i, 0)),
        out_shape=jax.ShapeDtypeStruct((r, n), out_dtype),
        compiler_params=_cparams(("parallel",), 40),
        name="matmul",
    )(a, b)


def _fill_window(win_ref, cur, prev, nxt, t):
    n = pl.program_id(1)
    last = pl.num_programs(1) - 1
    win_ref[HALO:HALO + t, :] = cur
    win_ref[0:HALO, :] = jnp.where(n > 0, prev, 0.0)
    win_ref[HALO + t:HALO + t + HALO, :] = jnp.where(n < last, nxt, 0.0)


def _dwconv_chunk(win_ref, t0, w_ref, ktaps):
    off = HALO - ktaps // 2
    nfull = -(-(off + ktaps) // 8) * 8
    w = win_ref[pl.ds(t0, CONV_CHUNK + nfull), :]
    acc = None
    for r in range(8):
        part = None
        for a in range(nfull // 8):
            j = 8 * a + r - off
            if 0 <= j < ktaps:
                term = w[8 * a:8 * a + CONV_CHUNK + 8] * w_ref[j:j + 1, :]
                part = term if part is None else part + term
        if part is not None:
            shifted = part[r:r + CONV_CHUNK]
            acc = shifted if acc is None else acc + shifted
    return acc


def _conformer_kernel(cur_ref, prev_ref, next_ref, cw_ref, cb_ref, lg_ref, lb_ref,
                      o_ref, win_ref, *, t):
    dc = o_ref.shape[-1]

    def glu(ref):
        blk = ref[...].astype(F32)
        return blk[:, :dc] * _sigmoid(blk[:, dc:])

    _fill_window(win_ref, glu(cur_ref), glu(prev_ref), glu(next_ref), t)

    def chunk(i, carry):
        t0 = pl.multiple_of(i * CONV_CHUNK, CONV_CHUNK)
        h = _dwconv_chunk(win_ref, t0, cw_ref, CONV_K) + cb_ref[...]
        mu = jnp.mean(h, axis=-1, keepdims=True)
        hc = h - mu
        y = hc * lax.rsqrt(jnp.mean(hc * hc, axis=-1, keepdims=True) + EPS)
        y = y * lg_ref[...] + lb_ref[...]
        o_ref[pl.ds(t0, CONV_CHUNK), :] = _silu(y).astype(BF16)
        return carry

    lax.fori_loop(0, t // CONV_CHUNK, chunk, 0)


def _halo_specs(width, col_block, s, t):
    nblk = t // HALO
    per_sample = s // HALO

    def prev_map(b, n):
        return (jnp.maximum(b * per_sample + n * nblk - 1, 0), col_block)

    def next_map(b, n):
        return (jnp.minimum(b * per_sample + (n + 1) * nblk, (b + 1) * per_sample - 1), col_block)

    return pl.BlockSpec((HALO, width), prev_map), pl.BlockSpec((HALO, width), next_map)


def _conformer(p, conv_w, conv_b, ln_g, ln_b, batch, s):
    t = min(512, s)
    nt = s // t
    prev_spec, next_spec = _halo_specs(2 * D_CONV, COL_A // (2 * D_CONV), s, t)
    vec = pl.BlockSpec((1, D_CONV), lambda b, n: (0, 0))
    return pl.pallas_call(
        functools.partial(_conformer_kernel, t=t),
        grid=(batch, nt),
        in_specs=[pl.BlockSpec((t, 2 * D_CONV), lambda b, n: (b * nt + n, COL_A // (2 * D_CONV))),
                  prev_spec, next_spec,
                  pl.BlockSpec((CONV_K, D_CONV), lambda b, n: (0, 0)),
                  vec, vec, vec],
        out_specs=pl.BlockSpec((t, D_CONV), lambda b, n: (b * nt + n, 0)),
        out_shape=jax.ShapeDtypeStruct((batch * s, D_CONV), BF16),
        scratch_shapes=[pltpu.VMEM((t + 2 * HALO, D_CONV), F32)],
        compiler_params=_cparams(("parallel", "parallel"), 40),
        name="conformer_conv",
    )(p, p, p, conv_w, conv_b.reshape(1, -1), ln_g.reshape(1, -1), ln_b.reshape(1, -1))


def _short_kernel(bg_ref, cg_ref, hv_ref, cgp_ref, hvp_ref, cgn_ref, hvn_ref, w_ref,
                  o_ref, win_ref, *, t):
    def prod(a_ref, b_ref):
        return a_ref[...].astype(F32) * b_ref[...].astype(F32)

    _fill_window(win_ref, prod(cg_ref, hv_ref), prod(cgp_ref, hvp_ref), prod(cgn_ref, hvn_ref), t)

    def chunk(i, carry):
        t0 = pl.multiple_of(i * CONV_CHUNK, CONV_CHUNK)
        h = _dwconv_chunk(win_ref, t0, w_ref, SHORT_K)
        bg = bg_ref[pl.ds(t0, CONV_CHUNK), :].astype(F32)
        o_ref[pl.ds(t0, CONV_CHUNK), :] = (bg * h).astype(BF16)
        return carry

    lax.fori_loop(0, t // CONV_CHUNK, chunk, 0)


def _short_conv(p, conv_w, batch, s):
    t = min(512, s)
    nt = s // t
    cb = COL_SHORT // D_SHORT
    cgp, cgn = _halo_specs(D_SHORT, cb + 1, s, t)
    hvp, hvn = _halo_specs(D_SHORT, cb + 2, s, t)

    def cur(k):
        return pl.BlockSpec((t, D_SHORT), lambda b, n: (b * nt + n, cb + k))

    return pl.pallas_call(
        functools.partial(_short_kernel, t=t),
        grid=(batch, nt),
        in_specs=[cur(0), cur(1), cur(2), cgp, hvp, cgn, hvn,
                  pl.BlockSpec((SHORT_K, D_SHORT), lambda b, n: (0, 0))],
        out_specs=pl.BlockSpec((t, D_SHORT), lambda b, n: (b * nt + n, 0)),
        out_shape=jax.ShapeDtypeStruct((batch * s, D_SHORT), BF16),
        scratch_shapes=[pltpu.VMEM((t + 2 * HALO, D_SHORT), F32)],
        compiler_params=_cparams(("parallel", "parallel"), 40),
        name="short_conv",
    )(p, p, p, p, p, p, p, conv_w)


def _rope(x, cos, sin):
    w = x.shape[1]
    lane = lax.broadcasted_iota(I32, x.shape, 1)
    swapped = jnp.where((lane & 31) < 16, pltpu.roll(x, w - 16, 1), pltpu.roll(x, 16, 1))
    return x * cos + swapped * sin


def _stack_heads(q, g):
    grp = N_Q_HEADS // N_KV_HEADS
    return jnp.concatenate(
        [q[:, (grp * g + i) * HEAD_DIM:(grp * g + i + 1) * HEAD_DIM] for i in range(grp)], axis=0)


def _softmax_pv(s, sink_col, v_ones, g):
    m = jnp.maximum(jnp.max(s, axis=1, keepdims=True), sink_col)
    o = _dot(jnp.exp2(s - m).astype(BF16), v_ones)
    den = o[:, LANES:LANES + 1] + jnp.exp2(sink_col - m)
    return o[:, g * HEAD_DIM:(g + 1) * HEAD_DIM] / den


def _with_ones(v):
    return jnp.concatenate([v, jnp.ones_like(v)], axis=1).astype(BF16)


def _unstack_heads(outs, nq):
    grp = N_Q_HEADS // N_KV_HEADS
    pieces = [o[i * nq:(i + 1) * nq] for o in outs for i in range(grp)]
    return jnp.concatenate(pieces, axis=1)


def _attn_kernel(q_ref, kv_ref, ckv_ref, cos_ref, sin_ref, bias_ref, sink_ref, o_ref, *, nb):
    n = pl.program_id(1)
    q0 = pl.multiple_of(n * BLOCK, BLOCK)
    sp = pl.multiple_of(jnp.maximum(n - 1, 0) * BLOCK, BLOCK)
    sn = pl.multiple_of(jnp.minimum(n + 1, nb - 1) * BLOCK, BLOCK)
    kvw = N_KV_HEADS * HEAD_DIM

    cq = cos_ref[pl.ds(q0, BLOCK), :]
    sq = sin_ref[pl.ds(q0, BLOCK), :]
    rep = N_Q_HEADS * HEAD_DIM // LANES
    q = _rope(q_ref[...].astype(F32), jnp.concatenate([cq] * rep, axis=1),
              jnp.concatenate([sq] * rep, axis=1)) * (HEAD_DIM ** -0.5 * LOG2E)

    def kblock(start):
        kvb = kv_ref[pl.ds(start, BLOCK), :].astype(F32)
        k = _rope(kvb[:, :kvw], cos_ref[pl.ds(start, BLOCK), :], sin_ref[pl.ds(start, BLOCK), :])
        return k, kvb[:, kvw:]

    kp, vp = kblock(sp)
    kc, vc = kblock(q0)
    kn, vn = kblock(sn)
    ckv = ckv_ref[...].astype(F32)
    k_all = jnp.concatenate([kp, kc, kn, ckv[:, :kvw]], axis=0)
    v_t = jnp.concatenate([blk.T for blk in (vp, vc, vn)]
                          + [ckv[i:i + BLOCK, kvw:].T for i in range(0, ckv.shape[0], BLOCK)], axis=1)
    ones = jnp.ones((HEAD_DIM, v_t.shape[1]), F32)

    bias_prev = bias_ref[:BLOCK, :] + jnp.where(n == 0, NEG_INF, 0.0)
    bias_next = bias_ref[BLOCK:, :] + jnp.where(n == nb - 1, NEG_INF, 0.0)

    grp = N_Q_HEADS // N_KV_HEADS
    heads = range(N_KV_HEADS)
    scores = []
    for g in heads:
        qs = _stack_heads(q, g).astype(BF16)
        kh = k_all[:, g * HEAD_DIM:(g + 1) * HEAD_DIM].astype(BF16)
        s = lax.dot_general(kh, qs, (((1,), (1,)), ((), ())), preferred_element_type=F32)
        scores.append(jnp.concatenate([s[:BLOCK] + bias_prev, s[BLOCK:2 * BLOCK],
                                       s[2 * BLOCK:3 * BLOCK] + bias_next, s[3 * BLOCK:]], axis=0))
    sinks = [sink_ref[g][:1, :] for g in heads]
    maxes = [jnp.maximum(jnp.max(scores[g], axis=0, keepdims=True), sinks[g]) for g in heads]
    exps = [jnp.exp2(scores[g] - maxes[g]).astype(BF16) for g in heads]
    outs = []
    for g in heads:
        lhs = jnp.concatenate([v_t[g * HEAD_DIM:(g + 1) * HEAD_DIM], ones], axis=0).astype(BF16)
        o_t = _dot(lhs, exps[g])
        den = o_t[HEAD_DIM:HEAD_DIM + 1] + jnp.exp2(sinks[g] - maxes[g])
        o_t = o_t * (1.0 / den)
        outs += [o_t[:, i * BLOCK:(i + 1) * BLOCK].T[:, :HEAD_DIM] for i in range(grp)]
    o_ref[...] = jnp.concatenate(outs, axis=1).astype(BF16)


def _band_bias():
    grp = N_Q_HEADS // N_KV_HEADS
    kj = jnp.arange(BLOCK)[:, None]
    qi = jnp.arange(BLOCK)[None, :]
    prev = jnp.where(qi <= kj, 0.0, NEG_INF)
    nxt = jnp.where(kj <= qi, 0.0, NEG_INF)
    return jnp.tile(jnp.concatenate([prev, nxt], axis=0).astype(F32), (1, grp))


def _sink_rows(sink):
    grp = N_Q_HEADS // N_KV_HEADS
    row = jnp.repeat(sink.astype(F32).reshape(N_KV_HEADS, grp) * LOG2E, BLOCK, axis=1)
    return jnp.broadcast_to(row[:, None, :], (N_KV_HEADS, 8, grp * BLOCK))


def _sink_cols(sink, nq):
    grp = N_Q_HEADS // N_KV_HEADS
    col = jnp.repeat(sink.astype(F32).reshape(N_KV_HEADS, grp) * LOG2E, nq, axis=1)
    return jnp.broadcast_to(col[:, :, None], (N_KV_HEADS, grp * nq, LANES))


def _latent_attention(p, pc, cos_t, sin_t, sink, batch, s, n_ctx):
    nb = s // BLOCK
    grp = N_Q_HEADS // N_KV_HEADS
    qw = N_Q_HEADS * HEAD_DIM
    kvw2 = 2 * N_KV_HEADS * HEAD_DIM
    bias = _band_bias()
    tab =pl.BlockSpec((s, LANES), lambda b, n: (0, 0))
    return pl.pallas_call(
        functools.partial(_attn_kernel, nb=nb),
        grid=(batch, nb),
        in_specs=[pl.BlockSpec((BLOCK, qw), lambda b, n: (b * nb + n, COL_Q // qw)),
                  pl.BlockSpec((s, kvw2), lambda b, n: (b, COL_KV // kvw2)),
                  pl.BlockSpec((n_ctx, kvw2), lambda b, n: (b, COL_KV // kvw2)),
                  tab, tab,
                  pl.BlockSpec(bias.shape, lambda b, n: (0, 0)),
                  pl.BlockSpec((N_KV_HEADS, 8, grp * BLOCK), lambda b, n: (0, 0, 0))],
        out_specs=pl.BlockSpec((BLOCK, qw), lambda b, n: (b * nb + n, 0)),
        out_shape=jax.ShapeDtypeStruct((batch * s, qw), BF16),
        compiler_params=_cparams(("parallel", "parallel"), 40),
        name="latent_attention",
    )(p, p, pc, cos_t, sin_t, bias, _sink_rows(sink))


def _cattn_kernel(q_ref, kv_ref, sink_ref, o_ref):
    nq = q_ref.shape[0]
    kvw = N_KV_HEADS * HEAD_DIM
    q = q_ref[...].astype(F32) * (HEAD_DIM ** -0.5 * LOG2E)
    kv = kv_ref[...].astype(F32)
    v_ones = _with_ones(kv[:, kvw:])
    outs = []
    for g in range(N_KV_HEADS):
        qs = _stack_heads(q, g).astype(BF16)
        kh = kv[:, g * HEAD_DIM:(g + 1) * HEAD_DIM].astype(BF16)
        s = lax.dot_general(qs, kh, (((1,), (1,)), ((), ())), preferred_element_type=F32)
        outs.append(_softmax_pv(s, sink_ref[g][:, :1], v_ones, g))
    o_ref[...] = _unstack_heads(outs, nq).astype(BF16)


def _context_attention(pc, sink, batch, n_ctx):
    grp = N_Q_HEADS // N_KV_HEADS
    qw = N_Q_HEADS * HEAD_DIM
    kvw2 = 2 * N_KV_HEADS * HEAD_DIM
    return pl.pallas_call(
        _cattn_kernel,
        grid=(batch,),
        in_specs=[pl.BlockSpec((n_ctx, qw), lambda b: (b, COL_Q // qw)),
                  pl.BlockSpec((n_ctx, kvw2), lambda b: (b, COL_KV // kvw2)),
                  pl.BlockSpec((N_KV_HEADS, grp * n_ctx, LANES), lambda b: (0, 0, 0))],
        out_specs=pl.BlockSpec((n_ctx, qw), lambda b: (b, 0)),
        out_shape=jax.ShapeDtypeStruct((batch * n_ctx, qw), BF16),
        compiler_params=_cparams(("parallel",), 40),
        name="context_attention",
    )(pc, pc, _sink_cols(sink, n_ctx))


FFT_TILE = 8


def _fft1_kernel(v_ref, w_ref, tc_ref, ts_ref, o_ref):
    df = D_FOURIER
    w = w_ref[...]
    n2 = w.shape[0] // 2
    for i in range(FFT_TILE):
        z = jnp.concatenate([v_ref[:, 2 * i * df:(2 * i + 1) * df],
                             v_ref[:, (2 * i + 1) * df:(2 * i + 2) * df]], axis=0)
        c = _dot(w, z)
        cr, ci = c[:n2], c[n2:]
        tc = jnp.concatenate([tc_ref[i]] * (df // LANES), axis=1)
        ts = jnp.concatenate([ts_ref[i]] * (df // LANES), axis=1)
        o_ref[i, :, :df] = (cr * tc - ci * ts).astype(BF16)
        o_ref[i, :, df:] = (cr * ts + ci * tc).astype(BF16)


def _fft2_kernel(y_ref, w_ref, o_ref):
    df = D_FOURIER
    w = w_ref[...]
    for i in range(FFT_TILE):
        y = jnp.concatenate([y_ref[:, 2 * i * df:(2 * i + 1) * df],
                             y_ref[:, (2 * i + 1) * df:(2 * i + 2) * df]], axis=0)
        o_ref[:, i * df:(i + 1) * df] = _dot(w, y).astype(BF16)


def _fft_split(s):
    n1 = 1 << ((s.bit_length() - 1) // 2)
    return n1, s // n1


def _fft_tables(s):
    n1, n2 = _fft_split(s)
    c2, s2 = _dft_tables(n2)
    w1 = jnp.concatenate([jnp.concatenate([c2, -s2], axis=1),
                          jnp.concatenate([s2, c2], axis=1)], axis=0).astype(BF16)
    c1, s1 = _dft_tables(n1)
    w2 = jnp.concatenate([c1, -s1], axis=1).astype(BF16)
    ang = (jnp.arange(n1)[:, None] * jnp.arange(n2)[None, :]).astype(F32) * (2.0 * math.pi / s)
    tc = jnp.broadcast_to(jnp.cos(ang)[:, :, None], (n1, n2, LANES))
    ts = jnp.broadcast_to(jnp.sin(ang)[:, :, None], (n1, n2, LANES))
    return w1, w2, tc, ts


def _position_dft(v, tables, batch, s):
    w1, w2, tc, ts = tables
    n1, n2 = _fft_split(s)
    df2 = 2 * D_FOURIER
    nt1 = n1 // FFT_TILE
    nt2 = n2 // FFT_TILE
    stage1 = pl.pallas_call(
        _fft1_kernel,
        grid=(batch, nt1),
        in_specs=[pl.BlockSpec((n2, FFT_TILE * df2), lambda b, j: (b, j)),
                  pl.BlockSpec(w1.shape, lambda b, j: (0, 0)),
                  pl.BlockSpec((FFT_TILE, n2, LANES), lambda b, j: (j, 0, 0)),
                  pl.BlockSpec((FFT_TILE, n2, LANES), lambda b, j: (j, 0, 0))],
        out_specs=pl.BlockSpec((FFT_TILE, n2, df2), lambda b, j: (b * nt1 + j, 0, 0)),
        out_shape=jax.ShapeDtypeStruct((batch * n1, n2, df2), BF16),
        compiler_params=_cparams(("parallel", "parallel"), 40),
        name="fft_stage1",
    )(v.reshape(batch * n2, n1 * df2), w1, tc, ts)
    out = pl.pallas_call(
        _fft2_kernel,
        grid=(batch, nt2),
        in_specs=[pl.BlockSpec((n1, FFT_TILE * df2), lambda b, j: (b, j)),
                  pl.BlockSpec(w2.shape, lambda b, j: (0, 0))],
        out_specs=pl.BlockSpec((n1, FFT_TILE * D_FOURIER), lambda b, j: (b, j)),
        out_shape=jax.ShapeDtypeStruct((batch * n1, n2 * D_FOURIER), BF16),
        compiler_params=_cparams(("parallel", "parallel"), 40),
        name="fft_stage2",
    )(stage1.reshape(batch * n1, n2 * df2), w2)
    return out.reshape(batch * s, D_FOURIER)


def _dft_tables(n):
    k = jnp.arange(n, dtype=I32)
    ang = ((k[:, None] * k[None, :]) % n).astype(F32) * (2.0 * math.pi / n)
    scale = n ** -0.5
    return jnp.cos(ang) * scale, jnp.sin(ang) * scale


def _channel_dft_matrix():
    cg = D_FOURIER // N_FOURIER_GROUPS
    c, s = _dft_tables(cg)
    eye = jnp.eye(N_FOURIER_GROUPS, dtype=F32)
    return jnp.concatenate([jnp.kron(eye, c), jnp.kron(eye, s)], axis=1).astype(BF16)


def _merge_kernel(ba_ref, bb_ref, bc_ref, bd_ref, gl_ref, gb_ref, wa_ref, wb_ref, wc_ref, wd_ref,
                  wo_ref, xs_ref, m2_ref, m3_ref, m4_ref, pg_ref, fg_ref, rw_ref,
                  xo_ref, hp_ref, aff_ref):
    d = xs_ref.shape[-1]
    y = None
    for i, (b_ref, w_ref) in enumerate(((ba_ref, wa_ref), (bb_ref, wb_ref),
                                        (bc_ref, wc_ref), (bd_ref, wd_ref))):
        gate = _tanh_gate(gl_ref[:, i * d:(i + 1) * d].astype(F32) + gb_ref[:, i * d:(i + 1) * d])
        term = gate * _dot(b_ref[...], w_ref[...])
        y = term if y is None else y + term
    z = _dot(y.astype(BF16), wo_ref[...])
    xs = xs_ref[...] + m2_ref[0] * _rms(z, pg_ref[...])
    xo_ref[...] = xs
    h = _rms(xs, fg_ref[...]) * (1.0 + m4_ref[0]) + m3_ref[0]
    hp_ref[...] = h.astype(BF16).astype(F32)
    logits = _dot3(h, rw_ref[...])
    lane = lax.broadcasted_iota(I32, logits.shape, 1)
    logits = jnp.where(lane < N_EXPERTS, logits, NEG_INF)
    e = jnp.exp(logits - jnp.max(logits, axis=1, keepdims=True))
    aff_ref[...] = e / jnp.sum(e, axis=1, keepdims=True)


def _merge(ba, bb, bc, bd, p, gate_b, wa, wb, wc, wd, wo, xs, m2, m3, m4, post_g, ffn_g, rw,
           rows_per_group):
    r, d = xs.shape
    tm = min(512, rows_per_group)
    tiles_per_group = rows_per_group // tm
    gw = N_BRANCH * d

    def rows(width):
        return pl.BlockSpec((tm, width), lambda i: (i, 0))

    def const(shape):
        return pl.BlockSpec(shape, lambda i: (0,) * len(shape))

    mod = pl.BlockSpec((1, 1, d), lambda i: (i // tiles_per_group, 0, 0))
    half = d // 2
    return pl.pallas_call(
        _merge_kernel,
        grid=(r // tm,),
        in_specs=[rows(half), rows(half), rows(half), rows(half),
                  pl.BlockSpec((tm, gw), lambda i: (i, COL_G // gw)),
                  const((1, gw)),
                  const((half, d)), const((half, d)), const((half, d)), const((half, d)),
                  const((d, d)),
                  rows(d), mod, mod, mod, const((1, d)), const((1, d)), const((d, LANES))],
        out_specs=[rows(d), rows(d), rows(LANES)],
        out_shape=[jax.ShapeDtypeStruct((r, d), F32),
                   jax.ShapeDtypeStruct((r, d), F32),
                   jax.ShapeDtypeStruct((r, LANES), F32)],
        compiler_params=_cparams(("parallel",), 48),
        name="merge_router",
    )(ba, bb, bc, bd, p, 0.5 * gate_b.reshape(1, gw), wa, wb, wc, wd, wo, xs, m2, m3, m4,
      post_g.reshape(1, d), ffn_g.reshape(1, d), rw)


def _route_kernel(aff_ref, tri_ref, tl_ref, idx_ref, *, cap):
    s = aff_ref.shape[0]
    nslot = idx_ref.shape[-1]
    aff = aff_ref[...]

    def as_float(bits):
        return lax.bitcast_convert_type(bits, F32)

    def count(mask):
        part = jnp.sum(jnp.where(mask, 1.0, 0.0).reshape(s // 64, 64, LANES), axis=0)
        return jnp.sum(part, axis=0, keepdims=True)

    def search(i, thr):
        cand = thr | lax.shift_left(jnp.int32(1), 30 - i)
        return jnp.where(count(aff >= as_float(cand)) >= cap, cand, thr)

    thr = lax.fori_loop(0, 31, search, jnp.zeros((1, LANES), I32))
    gt = aff >= as_float(jnp.maximum(thr + 1, MIN_NORMAL_BITS))
    eq = (aff >= as_float(thr)) & jnp.logical_not(gt)
    need = cap - count(gt)

    tri = tri_ref[...]

    def cumsum_excl(m):
        off = jnp.zeros((1, LANES), F32)
        outs = []
        for c in range(s // LANES):
            mc = m[c * LANES:(c + 1) * LANES]
            cs = _dot(tri, mc.astype(BF16))
            outs.append(cs - mc + off)
            off = off + cs[LANES - 1:LANES, :]
        return jnp.concatenate(outs, axis=0)

    eq_f = jnp.where(eq, 1.0, 0.0)
    sel = gt | (eq & (cumsum_excl(eq_f) < need))
    sel_f = jnp.where(sel, 1.0, 0.0)
    pos = jnp.where(sel, cumsum_excl(sel_f), -1.0)

    slot = lax.broadcasted_iota(I32, (s, nslot), 1).astype(F32)
    tl = tl_ref[...]
    for e in range(N_EXPERTS):
        onehot = jnp.where(pos[:, e:e + 1] == slot, 1.0, 0.0).astype(BF16)
        res = _dot(tl, onehot)
        idx_ref[0, e:e + 1, :] = (res[0:1] * 64.0 + res[1:2] + 0.5).astype(I32)


def _route(aff, batch, s, cap):
    nslot = max(cap, LANES)
    tri = (jnp.arange(LANES)[:, None] >= jnp.arange(LANES)[None, :]).astype(BF16)
    t = jnp.arange(s)
    tl = jnp.zeros((8, s), F32).at[0].set(t // 64).at[1].set(t % 64).astype(BF16)
    idx = pl.pallas_call(
        functools.partial(_route_kernel, cap=cap),
        grid=(batch,),
        in_specs=[pl.BlockSpec((s, LANES), lambda b: (b, 0)),
                  pl.BlockSpec((LANES, LANES), lambda b: (0, 0)),
                  pl.BlockSpec((8, s), lambda b: (0, 0))],
        out_specs=pl.BlockSpec((1, N_EXPERTS, nslot), lambda b: (b, 0, 0)),
        out_shape=jax.ShapeDtypeStruct((batch, N_EXPERTS, nslot), I32),
        compiler_params=_cparams(("parallel",), 48),
        name="route_topk",
    )(aff, tri, tl)
    return idx[:, :, :cap].reshape(-1)


def _gather_kernel(idx_ref, h_ref, aff_ref, xg_ref, val_ref, g_ref, ga_ref, *, cap):
    b = pl.program_id(0)
    e = pl.program_id(1)
    base = (b * N_EXPERTS + e) * cap

    group = 8

    def body(jg, carry):
        j0 = pl.multiple_of(jg * group, group)
        for k in range(group):
            t = idx_ref[base + j0 + k]
            g_ref[pl.ds(j0 + k, 1), :] = h_ref[pl.ds(t, 1), :]
            ga_ref[pl.ds(j0 + k, 1), :] = aff_ref[pl.ds(t, 1), :]
        return carry

    lax.fori_loop(0, cap // group, body, 0)
    xg_ref[...] = g_ref[...].astype(BF16)
    lane = lax.broadcasted_iota(I32, ga_ref.shape, 1)
    v = jnp.sum(jnp.where(lane == e, ga_ref[...], 0.0), axis=1, keepdims=True)
    val_ref[...] = jnp.broadcast_to(v, val_ref.shape)


def _gather(idx, hp, aff, batch, s, cap):
    d = hp.shape[1]
    grid_spec = pltpu.PrefetchScalarGridSpec(
        num_scalar_prefetch=1,
        grid=(batch, N_EXPERTS),
        in_specs=[pl.BlockSpec((s, d), lambda b, e, idx: (b, 0)),
                  pl.BlockSpec((s, LANES), lambda b, e, idx: (b, 0))],
        out_specs=[pl.BlockSpec((cap, d), lambda b, e, idx: (e * batch + b, 0)),
                   pl.BlockSpec((cap, LANES), lambda b, e, idx: (e * batch + b, 0))],
        scratch_shapes=[pltpu.VMEM((cap, d), F32), pltpu.VMEM((cap, LANES), F32)])
    rows = N_EXPERTS * batch * cap
    return pl.pallas_call(
        functools.partial(_gather_kernel, cap=cap),
        grid_spec=grid_spec,
        out_shape=[jax.ShapeDtypeStruct((rows, d), BF16),
                   jax.ShapeDtypeStruct((rows, LANES), F32)],
        compiler_params=_cparams(("arbitrary", "arbitrary"), 52),
        name="moe_gather",
    )(idx, hp, aff)


def _ffn_kernel(x_ref, v_ref, w1_ref, w3_ref, w2_ref, y_ref, w1b, w3b, w2b):
    @pl.when(pl.program_id(1) == 0)
    def _():
        w1b[...] = w1_ref[0, 0].astype(BF16)
        w3b[...] = w3_ref[0, 0].astype(BF16)
        w2b[...] = w2_ref[0, 0].astype(BF16)

    x = x_ref[...]
    hid = _silu(_dot(x, w1b[...])) * _dot(x, w3b[...])
    y = _dot(hid.astype(BF16), w2b[...])
    y_ref[...] = y * jnp.concatenate([v_ref[...]] * (y.shape[1] // LANES), axis=1)


def _ffn(xg, vals, w1, w3, w2, layer, rows_per_expert):
    rows, d = xg.shape
    f = w1.shape[-1]
    tm = min(512, rows_per_expert)
    nt = rows_per_expert // tm

    def wspec(a, c):
        return pl.BlockSpec((1, 1, a, c), lambda e, m: (layer, e, 0, 0))

    return pl.pallas_call(
        _ffn_kernel,
        grid=(N_EXPERTS, nt),
        in_specs=[pl.BlockSpec((tm, d), lambda e, m: (e * nt + m, 0)),
                  pl.BlockSpec((tm, LANES), lambda e, m: (e * nt + m, 0)),
                  wspec(d, f), wspec(d, f), wspec(f, d)],
        out_specs=pl.BlockSpec((tm, d), lambda e, m: (e * nt + m, 0)),
        out_shape=jax.ShapeDtypeStruct((rows, d), F32),
        scratch_shapes=[pltpu.VMEM((d, f), BF16), pltpu.VMEM((d, f), BF16), pltpu.VMEM((f, d), BF16)],
        compiler_params=_cparams(("parallel", "arbitrary"), 56),
        name="expert_ffn",
    )(xg, vals, w1, w3, w2)


def _combine_kernel(idx_ref, y_ref, xs_ref, m5_ref, g_ref, o_ref, acc_ref, *, cap, tf):
    b = pl.program_id(0)
    step = pl.program_id(1)

    @pl.when(step == 0)
    def _():
        acc_ref[...] = jnp.zeros_like(acc_ref)

    @pl.when(step < N_EXPERTS)
    def _():
        base = (b * N_EXPERTS + step) * cap
        group = 8

        def body(jg, carry):
            j0 = pl.multiple_of(jg * group, group)
            toks = [idx_ref[base + j0 + k] for k in range(group)]
            rows = [acc_ref[pl.ds(tk, 1), :] for tk in toks]
            for k in range(group):
                acc_ref[pl.ds(toks[k], 1), :] = rows[k] + y_ref[pl.ds(j0 + k, 1), :]
            return carry

        lax.fori_loop(0, cap // group, body, 0)

    @pl.when(step >= N_EXPERTS)
    def _():
        r0 = pl.multiple_of((step - N_EXPERTS) * tf, tf)
        o_ref[...] = xs_ref[...] + m5_ref[0] * _rms(acc_ref[pl.ds(r0, tf), :], g_ref[...])


def _combine(idx, y, xs, m5, post_g, batch, s, cap, shared_mod):
    d = xs.shape[1]
    tf = min(512, s)
    nfin = s // tf

    def chunk_map(b, st, idx):
        return (b * nfin + jnp.maximum(st - N_EXPERTS, 0), 0)

    grid_spec = pltpu.PrefetchScalarGridSpec(
        num_scalar_prefetch=1,
        grid=(batch, N_EXPERTS + nfin),
        in_specs=[pl.BlockSpec((cap, d), lambda b, st, idx: (jnp.minimum(st, N_EXPERTS - 1) * batch + b, 0)),
                  pl.BlockSpec((tf, d), chunk_map),
                  pl.BlockSpec((1, 1, d), lambda b, st, idx: (0 if shared_mod else b, 0, 0)),
                  pl.BlockSpec((1, d), lambda b, st, idx: (0, 0))],
        out_specs=pl.BlockSpec((tf, d), chunk_map),
        scratch_shapes=[pltpu.VMEM((s, d), F32)])
    return pl.pallas_call(
        functools.partial(_combine_kernel, cap=cap, tf=tf),
        grid_spec=grid_spec,
        out_shape=jax.ShapeDtypeStruct(xs.shape, F32),
        compiler_params=_cparams(("arbitrary", "arbitrary"), 48),
        name="moe_combine",
    )(idx, y, xs, m5, post_g.reshape(1, d))


def _prep_w_in(w):
    d = w.shape[0]
    kv = 2 * N_KV_HEADS * HEAD_DIM
    o_q = 2 * D_CONV + 3 * D_SHORT
    o_k = o_q + N_Q_HEADS * HEAD_DIM
    o_f = o_k + kv
    o_g = o_f + D_FOURIER
    parts = [w[:, :o_k], w[:, o_f:o_g], w[:, o_k:o_f],
             jnp.zeros((d, COL_G - COL_KV - kv), w.dtype), 0.5 * w[:, o_g:]]
    return jnp.concatenate(parts, axis=1).astype(BF16)


def _rope_tables(s):
    t = jnp.arange(s)
    row = (t // GRID_W).astype(F32)
    col = (t % GRID_W).astype(F32)
    nf = HEAD_DIM // 4
    inv = ROPE_BASE ** (-jnp.arange(nf, dtype=F32) / nf)
    ar = row[:, None] * inv
    ac = col[:, None] * inv
    cos = jnp.concatenate([jnp.cos(ar), jnp.cos(ar), jnp.cos(ac), jnp.cos(ac)], axis=1)
    sin = jnp.concatenate([-jnp.sin(ar), jnp.sin(ar), -jnp.sin(ac), jnp.sin(ac)], axis=1)
    rep = LANES // HEAD_DIM
    return jnp.tile(cos, (1, rep)), jnp.tile(sin, (1, rep))


def _moe(hp, aff, xs, m5, post_g, w1, w3, w2, layer, batch, s, shared_mod):
    cap = CAPACITY_FACTOR * s // N_EXPERTS
    idx = _route(aff, batch, s, cap)
    xg, vals = _gather(idx, hp, aff, batch, s, cap)
    y = _ffn(xg, vals, w1, w3, w2, layer, batch * cap)
    return _combine(idx, y, xs, m5, post_g, batch, s, cap, shared_mod)


def kernel(x, c, ctx, c_ctx, ada_w, ada_b, pre_mix_g, post_mix_g, pre_ffn_g, post_ffn_g, w_in, gate_b, conv_a_w, conv_a_b, ln_a_g, ln_a_b, w_a_out, conv_b_w, w_b_out, sink, w_c_out, w_d_out, w_o, router_w, exp_w1, exp_w3, exp_w2):
    batch, s, d = x.shape
    n_ctx = ctx.shape[1]
    depth = ada_w.shape[0]

    cvec = jnp.zeros((16, d), F32).at[:batch].set(c).at[batch].set(c_ctx)
    mod = _ada(cvec, ada_w, ada_b)
    cos_t, sin_t = _rope_tables(s)
    bd = _channel_dft_matrix()
    dft_x = _fft_tables(s)
    dft_c = _fft_tables(n_ctx)

    xs = x.reshape(batch * s, d)
    cs = ctx.reshape(batch * n_ctx, d)
    for l in range(depth):
        last = l == depth - 1
        mx = [mod[l, :batch, k * d:(k + 1) * d].reshape(batch, 1, d) for k in range(6)]
        mc = [mod[l, batch:batch + 1, k * d:(k + 1) * d].reshape(1, 1, d) for k in range(6)]
        g_pre = pre_mix_g[l].reshape(1, d)
        w = _prep_w_in(w_in[l])
        wa, wb, wc, wd = ((0.5 * t[l]).astype(BF16) for t in (w_a_out, w_b_out, w_c_out, w_d_out))
        wo = w_o[l].astype(BF16)
        rw = jnp.zeros((d, LANES), F32).at[:, :N_EXPERTS].set(router_w[l])

        p = _inproj(xs, mx[0], mx[1], g_pre, w, s)
        pc = _inproj(cs, mc[0], mc[1], g_pre, w, batch * n_ctx)

        def mixer(pp, att, seq, tables, xres, m, rows_per_group):
            ba = _conformer(pp, conv_a_w[l], conv_a_b[l], ln_a_g[l], ln_a_b[l], batch, seq)
            bb = _short_conv(pp, conv_b_w[l], batch, seq)
            v = _mm(pp, bd, a_cols=COL_F)
            bf = _position_dft(v, tables, batch, seq)
            return _merge(ba, bb, att, bf, pp, gate_b[l], wa, wb, wc, wd, wo, xres,
                          m[2], m[3], m[4], post_mix_g[l], pre_ffn_g[l], rw, rows_per_group)

        att_x = _latent_attention(p, pc, cos_t, sin_t, sink[l], batch, s, n_ctx)
        xs, hp, aff = mixer(p, att_x, s, dft_x, xs, mx, s)
        xs = _moe(hp, aff, xs, mx[5], post_ffn_g[l], exp_w1, exp_w3, exp_w2, l, batch, s, False)
        if not last:
            att_c = _context_attention(pc, sink[l], batch, n_ctx)
            cs, hpc, affc = mixer(pc, att_c, n_ctx, dft_c, cs, mc, batch * n_ctx)
            cs = _moe(hpc, affc, cs, mc[5], post_ffn_g[l], exp_w1, exp_w3, exp_w2, l, batch, n_ctx, True)
    return xs.reshape(batch, s, d)
```

```python
import functools
import math

import jax
import jax.numpy as jnp
from jax import lax
from jax.experimental import pallas as pl
from jax.experimental.pallas import tpu as pltpu

F32 = jnp.float32
BF16 = jnp.bfloat16
I32 = jnp.int32

D_MODEL = 1024
GRID_W = 64
D_CONV = 512
CONV_K = 31
D_SHORT = 512
SHORT_K = 3
N_Q_HEADS = 8
N_KV_HEADS = 2
HEAD_DIM = 64
BLOCK = 128
D_FOURIER = 512
N_FOURIER_GROUPS = 4
N_BRANCH = 4
N_EXPERTS = 16
CAPACITY_FACTOR = 2
D_EXPERT = 1024
ROPE_BASE = 10000.0
EPS = 1e-6
NEG_INF = -1e30
LOG2E = 1.4426950408889634
MIN_NORMAL_BITS = 0x00800000

LANES = 128
SUBLANES = 8
HALO = 16
CONV_CHUNK = 64

COL_A = 0
COL_SHORT = 1024
COL_Q = 2560
COL_F = 3072
COL_KV = 3584
COL_G = 4096
N_PROJ = 8192


def _cparams(sem, vmem_mb):
    return pltpu.CompilerParams(dimension_semantics=sem,
                                vmem_limit_bytes=vmem_mb * 1024 * 1024)


def _sigmoid(x):
    return 0.5 * jnp.tanh(0.5 * x) + 0.5


def _tanh_gate(half_x):
    return jnp.tanh(half_x) + 1.0


def _silu(x):
    return x * _sigmoid(x)


def _rms(x, g):
    return x * lax.rsqrt(jnp.mean(x * x, axis=-1, keepdims=True) + EPS) * g


def _split_bf16(x):
    hi = x.astype(BF16)
    lo = (x - hi.astype(F32)).astype(BF16)
    return hi, lo


def _dot(a, b):
    return jnp.dot(a, b, preferred_element_type=F32)


def _dot3(a, b):
    ah, al = _split_bf16(a)
    bh, bl = _split_bf16(b)
    return _dot(ah, bh) + _dot(ah, bl) + _dot(al, bh)


def _ada_kernel(c_ref, w_ref, b_ref, o_ref):
    c = c_ref[...]
    o_ref[0] = _dot3(_silu(c), w_ref[0]) + b_ref[0]


def _ada(cvec, ada_w, ada_b):
    nl, d, n6 = ada_w.shape
    rows = cvec.shape[0]
    return pl.pallas_call(
        _ada_kernel,
        grid=(nl, n6 // d),
        in_specs=[pl.BlockSpec((rows, d), lambda l, j: (0, 0)),
                  pl.BlockSpec((1, d, d), lambda l, j: (l, 0, j)),
                  pl.BlockSpec((1, 1, d), lambda l, j: (l, 0, j))],
        out_specs=pl.BlockSpec((1, rows, d), lambda l, j: (l, 0, j)),
        out_shape=jax.ShapeDtypeStruct((nl, rows, n6), F32),
        compiler_params=_cparams(("parallel", "parallel"), 40),
        name="ada_mod",
    )(cvec, ada_w, ada_b.reshape(nl, 1, n6))


def _inproj_kernel(x_ref, sh_ref, sc_ref, g_ref, w_ref, o_ref, h_ref):
    @pl.when(pl.program_id(1) == 0)
    def _():
        y = _rms(x_ref[...], g_ref[...])
        h_ref[...] = (y * (1.0 + sc_ref[0]) + sh_ref[0]).astype(BF16)

    o_ref[...] = _dot(h_ref[...], w_ref[...]).astype(BF16)


def _inproj(x2d, shift, scale, gain, w, rows_per_group):
    r, d = x2d.shape
    n = w.shape[1]
    tm = min(2048, rows_per_group)
    tn = 1024
    tiles_per_group = rows_per_group // tm
    mod_spec = pl.BlockSpec((1, 1, d), lambda i, j: (i // tiles_per_group, 0, 0))
    return pl.pallas_call(
        _inproj_kernel,
        grid=(r // tm, n // tn),
        in_specs=[pl.BlockSpec((tm, d), lambda i, j: (i, 0)),
                  mod_spec, mod_spec,
                  pl.BlockSpec((1, d), lambda i, j: (0, 0)),
                  pl.BlockSpec((d, tn), lambda i, j: (0, j))],
        out_specs=pl.BlockSpec((tm, tn), lambda i, j: (i, j)),
        out_shape=jax.ShapeDtypeStruct((r, n), BF16),
        scratch_shapes=[pltpu.VMEM((tm, d), BF16)],
        compiler_params=_cparams(("parallel", "arbitrary"), 48),
        name="inproj",
    )(x2d, shift, scale, gain, w)


def _mm_kernel(a_ref, b_ref, o_ref):
    o_ref[...] = _dot(a_ref[...], b_ref[...]).astype(o_ref.dtype)


def _mm(a, b, *, a_cols=None, out_dtype=BF16, tm=1024):
    k, n = b.shape
    r = a.shape[0]
    cb = 0 if a_cols is None else a_cols // k
    tm = min(tm, r)
    return pl.pallas_call(
        _mm_kernel,
        grid=(r // tm,),
        in_specs=[pl.BlockSpec((tm, k), lambda i: (i, cb)),
                  pl.BlockSpec((k, n), lambda i: (0, 0))],
        out_specs=pl.BlockSpec((tm, n), lambda i: (i, 0)),
        out_shape=jax.ShapeDtypeStruct((r, n), out_dtype),
        compiler_params=_cparams(("parallel",), 40),
        name="matmul",
    )(a, b)


def _fill_window(win_ref, cur, prev, nxt, t):
    n = pl.program_id(1)
    last = pl.num_programs(1) - 1
    win_ref[HALO:HALO + t, :] = cur
    win_ref[0:HALO, :] = jnp.where(n > 0, prev, 0.0)
    win_ref[HALO + t:HALO + t + HALO, :] = jnp.where(n < last, nxt, 0.0)


def _dwconv_chunk(win_ref, t0, w_ref, ktaps):
    off = HALO - ktaps // 2
    nfull = -(-(off + ktaps) // 8) * 8
    w = win_ref[pl.ds(t0, CONV_CHUNK + nfull), :]
    acc = None
    for r in range(8):
        part = None
        for a in range(nfull // 8):
            j = 8 * a + r - off
            if 0 <= j < ktaps:
                term = w[8 * a:8 * a + CONV_CHUNK + 8] * w_ref[j:j + 1, :]
                part = term if part is None else part + term
        if part is not None:
            shifted = part[r:r + CONV_CHUNK]
            acc = shifted if acc is None else acc + shifted
    return acc


def _conformer_kernel(cur_ref, prev_ref, next_ref, cw_ref, cb_ref, lg_ref, lb_ref,
                      o_ref, win_ref, *, t):
    dc = o_ref.shape[-1]

    def glu(ref):
        blk = ref[...].astype(F32)
        return blk[:, :dc] * _sigmoid(blk[:, dc:])

    _fill_window(win_ref, glu(cur_ref), glu(prev_ref), glu(next_ref), t)

    def chunk(i, carry):
        t0 = pl.multiple_of(i * CONV_CHUNK, CONV_CHUNK)
        h = _dwconv_chunk(win_ref, t0, cw_ref, CONV_K) + cb_ref[...]
        mu = jnp.mean(h, axis=-1, keepdims=True)
        hc = h - mu
        y = hc * lax.rsqrt(jnp.mean(hc * hc, axis=-1, keepdims=True) + EPS)
        y = y * lg_ref[...] + lb_ref[...]
        o_ref[pl.ds(t0, CONV_CHUNK), :] = _silu(y).astype(BF16)
        return carry

    lax.fori_loop(0, t // CONV_CHUNK, chunk, 0)


def _halo_specs(width, col_block, s, t):
    nblk = t // HALO
    per_sample = s // HALO

    def prev_map(b, n):
        return (jnp.maximum(b * per_sample + n * nblk - 1, 0), col_block)

    def next_map(b, n):
        return (jnp.minimum(b * per_sample + (n + 1) * nblk, (b + 1) * per_sample - 1), col_block)

    return pl.BlockSpec((HALO, width), prev_map), pl.BlockSpec((HALO, width), next_map)


def _conformer(p, conv_w, conv_b, ln_g, ln_b, batch, s):
    t = min(512, s)
    nt = s // t
    prev_spec, next_spec = _halo_specs(2 * D_CONV, COL_A // (2 * D_CONV), s, t)
    vec = pl.BlockSpec((1, D_CONV), lambda b, n: (0, 0))
    return pl.pallas_call(
        functools.partial(_conformer_kernel, t=t),
        grid=(batch, nt),
        in_specs=[pl.BlockSpec((t, 2 * D_CONV), lambda b, n: (b * nt + n, COL_A // (2 * D_CONV))),
                  prev_spec, next_spec,
                  pl.BlockSpec((CONV_K, D_CONV), lambda b, n: (0, 0)),
                  vec, vec, vec],
        out_specs=pl.BlockSpec((t, D_CONV), lambda b, n: (b * nt + n, 0)),
        out_shape=jax.ShapeDtypeStruct((batch * s, D_CONV), BF16),
        scratch_shapes=[pltpu.VMEM((t + 2 * HALO, D_CONV), F32)],
        compiler_params=_cparams(("parallel", "parallel"), 40),
        name="conformer_conv",
    )(p, p, p, conv_w, conv_b.reshape(1, -1), ln_g.reshape(1, -1), ln_b.reshape(1, -1))


def _short_kernel(bg_ref, cg_ref, hv_ref, cgp_ref, hvp_ref, cgn_ref, hvn_ref, w_ref,
                  o_ref, win_ref, *, t):
    def prod(a_ref, b_ref):
        return a_ref[...].astype(F32) * b_ref[...].astype(F32)

    _fill_window(win_ref, prod(cg_ref, hv_ref), prod(cgp_ref, hvp_ref), prod(cgn_ref, hvn_ref), t)

    def chunk(i, carry):
        t0 = pl.multiple_of(i * CONV_CHUNK, CONV_CHUNK)
        h = _dwconv_chunk(win_ref, t0, w_ref, SHORT_K)
        bg = bg_ref[pl.ds(t0, CONV_CHUNK), :].astype(F32)
        o_ref[pl.ds(t0, CONV_CHUNK), :] = (bg * h).astype(BF16)
        return carry

    lax.fori_loop(0, t // CONV_CHUNK, chunk, 0)


def _short_conv(p, conv_w, batch, s):
    t = min(512, s)
    nt = s // t
    cb = COL_SHORT // D_SHORT
    cgp, cgn = _halo_specs(D_SHORT, cb + 1, s, t)
    hvp, hvn = _halo_specs(D_SHORT, cb + 2, s, t)

    def cur(k):
        return pl.BlockSpec((t, D_SHORT), lambda b, n: (b * nt + n, cb + k))

    return pl.pallas_call(
        functools.partial(_short_kernel, t=t),
        grid=(batch, nt),
        in_specs=[cur(0), cur(1), cur(2), cgp, hvp, cgn, hvn,
                  pl.BlockSpec((SHORT_K, D_SHORT), lambda b, n: (0, 0))],
        out_specs=pl.BlockSpec((t, D_SHORT), lambda b, n: (b * nt + n, 0)),
        out_shape=jax.ShapeDtypeStruct((batch * s, D_SHORT), BF16),
        scratch_shapes=[pltpu.VMEM((t + 2 * HALO, D_SHORT), F32)],
        compiler_params=_cparams(("parallel", "parallel"), 40),
        name="short_conv",
    )(p, p, p, p, p, p, p, conv_w)


def _rope(x, cos, sin):
    w = x.shape[1]
    lane = lax.broadcasted_iota(I32, x.shape, 1)
    swapped = jnp.where((lane & 31) < 16, pltpu.roll(x, w - 16, 1), pltpu.roll(x, 16, 1))
    return x * cos + swapped * sin


def _stack_heads(q, g):
    grp = N_Q_HEADS // N_KV_HEADS
    return jnp.concatenate(
        [q[:, (grp * g + i) * HEAD_DIM:(grp * g + i + 1) * HEAD_DIM] for i in range(grp)], axis=0)


def _softmax_pv(s, sink_col, v_ones, g):
    m = jnp.maximum(jnp.max(s, axis=1, keepdims=True), sink_col)
    o = _dot(jnp.exp2(s - m).astype(BF16), v_ones)
    den = o[:, LANES:LANES + 1] + jnp.exp2(sink_col - m)
    return o[:, g * HEAD_DIM:(g + 1) * HEAD_DIM] / den


def _with_ones(v):
    return jnp.concatenate([v, jnp.ones_like(v)], axis=1).astype(BF16)


def _unstack_heads(outs, nq):
    grp = N_Q_HEADS // N_KV_HEADS
    pieces = [o[i * nq:(i + 1) * nq] for o in outs for i in range(grp)]
    return jnp.concatenate(pieces, axis=1)


def _attn_kernel(q_ref, kv_ref, ckv_ref, cos_ref, sin_ref, bias_ref, sink_ref, o_ref, *, nb, qb):
    first = pl.program_id(1) * qb
    kvw = N_KV_HEADS * HEAD_DIM
    grp = N_Q_HEADS // N_KV_HEADS
    rep = N_Q_HEADS * HEAD_DIM // LANES

    q0 = pl.multiple_of(first * BLOCK, BLOCK)
    cq = cos_ref[pl.ds(q0, qb * BLOCK), :]
    sq = sin_ref[pl.ds(q0, qb * BLOCK), :]
    q = _rope(q_ref[...].astype(F32), jnp.concatenate([cq] * rep, axis=1),
              jnp.concatenate([sq] * rep, axis=1)) * (HEAD_DIM ** -0.5 * LOG2E)

    def kblock(j):
        start = pl.multiple_of(jnp.clip(first + j, 0, nb - 1) * BLOCK, BLOCK)
        kvb = kv_ref[pl.ds(start, BLOCK), :].astype(F32)
        k = _rope(kvb[:, :kvw], cos_ref[pl.ds(start, BLOCK), :], sin_ref[pl.ds(start, BLOCK), :])
        return k, kvb[:, kvw:].T

    blocks = [kblock(j) for j in range(-1, qb + 1)]
    ckv = ckv_ref[...].astype(F32)
    ctx_k = ckv[:, :kvw]
    ctx_vt = [ckv[i:i + BLOCK, kvw:].T for i in range(0, ckv.shape[0], BLOCK)]
    ones = jnp.ones((HEAD_DIM, 3 * BLOCK + ckv.shape[0]), F32)
    sinks = [sink_ref[g][:1, :] for g in range(N_KV_HEADS)]

    chains = [(i, g) for i in range(qb) for g in range(N_KV_HEADS)]
    scores = []
    for i, g in chains:
        bias_prev = bias_ref[:BLOCK, :] + jnp.where(first + i == 0, NEG_INF, 0.0)
        bias_next = bias_ref[BLOCK:, :] + jnp.where(first + i == nb - 1, NEG_INF, 0.0)
        qs = _stack_heads(q[i * BLOCK:(i + 1) * BLOCK], g).astype(BF16)
        k_all = jnp.concatenate([blocks[i][0], blocks[i + 1][0], blocks[i + 2][0], ctx_k], axis=0)
        kh = k_all[:, g * HEAD_DIM:(g + 1) * HEAD_DIM].astype(BF16)
        s = lax.dot_general(kh, qs, (((1,), (1,)), ((), ())), preferred_element_type=F32)
        scores.append(jnp.concatenate([s[:BLOCK] + bias_prev, s[BLOCK:2 * BLOCK],
                                       s[2 * BLOCK:3 * BLOCK] + bias_next, s[3 * BLOCK:]], axis=0))
    maxes = [jnp.maximum(jnp.max(s, axis=0, keepdims=True), sinks[g]) for s, (_, g) in zip(scores, chains)]
    exps = [jnp.exp2(s - m).astype(BF16) for s, m in zip(scores, maxes)]
    outs = [[] for _ in range(qb)]
    for (i, g), e, m in zip(chains, exps, maxes):
        v_t = jnp.concatenate([blocks[i][1], blocks[i + 1][1], blocks[i + 2][1]] + ctx_vt, axis=1)
        lhs = jnp.concatenate([v_t[g * HEAD_DIM:(g + 1) * HEAD_DIM], ones], axis=0).astype(BF16)
        o_t = _dot(lhs, e)
        den = o_t[HEAD_DIM:HEAD_DIM + 1] + jnp.exp2(sinks[g] - m)
        o_t = o_t * (1.0 / den)
        outs[i] += [o_t[:, h * BLOCK:(h + 1) * BLOCK].T[:, :HEAD_DIM] for h in range(grp)]
    for i in range(qb):
        o_ref[i * BLOCK:(i + 1) * BLOCK, :] = jnp.concatenate(outs[i], axis=1).astype(BF16)


def _band_bias():
    grp = N_Q_HEADS // N_KV_HEADS
    kj = jnp.arange(BLOCK)[:, None]
    qi = jnp.arange(BLOCK)[None, :]
    prev = jnp.where(qi <= kj, 0.0, NEG_INF)
    nxt = jnp.where(kj <= qi, 0.0, NEG_INF)
    return jnp.tile(jnp.concatenate([prev, nxt], axis=0).astype(F32), (1, grp))


def _sink_rows(sink):
    grp = N_Q_HEADS // N_KV_HEADS
    row = jnp.repeat(sink.astype(F32).reshape(N_KV_HEADS, grp) * LOG2E, BLOCK, axis=1)
    return jnp.broadcast_to(row[:, None, :], (N_KV_HEADS, 8, grp * BLOCK))


def _sink_cols(sink, nq):
    grp = N_Q_HEADS // N_KV_HEADS
    col = jnp.repeat(sink.astype(F32).reshape(N_KV_HEADS, grp) * LOG2E, nq, axis=1)
    return jnp.broadcast_to(col[:, :, None], (N_KV_HEADS, grp * nq, LANES))


def _latent_attention(p, pc, cos_t, sin_t, sink, batch, s, n_ctx):
    nb = s // BLOCK
    grp = N_Q_HEADS // N_KV_HEADS
    qw = N_Q_HEADS * HEAD_DIM
    kvw2 = 2 * N_KV_HEADS * HEAD_DIM
    bias = _band_bias()
    qb = 2 if nb % 2 == 0 else 1
    ns = nb // qb
    tab = pl.BlockSpec((s, LANES), lambda b, n: (0, 0))
    return pl.pallas_call(
        functools.partial(_attn_kernel, nb=nb, qb=qb),
        grid=(batch, ns),
        in_specs=[pl.BlockSpec((qb * BLOCK, qw), lambda b, n: (b * ns + n, COL_Q // qw)),
                  pl.BlockSpec((s, kvw2), lambda b, n: (b, COL_KV // kvw2)),
                  pl.BlockSpec((n_ctx, kvw2), lambda b, n: (b, COL_KV // kvw2)),
                  tab, tab,
                  pl.BlockSpec(bias.shape, lambda b, n: (0, 0)),
                  pl.BlockSpec((N_KV_HEADS, 8, grp * BLOCK), lambda b, n: (0, 0, 0))],
        out_specs=pl.BlockSpec((qb * BLOCK, qw), lambda b, n: (b * ns + n, 0)),
        out_shape=jax.ShapeDtypeStruct((batch * s, qw), BF16),
        compiler_params=_cparams(("parallel", "parallel"), 40),
        name="latent_attention",
    )(p, p, pc, cos_t, sin_t, bias, _sink_rows(sink))


def _cattn_kernel(q_ref, kv_ref, sink_ref, o_ref):
    nq = q_ref.shape[0]
    kvw = N_KV_HEADS * HEAD_DIM
    q = q_ref[...].astype(F32) * (HEAD_DIM ** -0.5 * LOG2E)
    kv = kv_ref[...].astype(F32)
    v_ones = _with_ones(kv[:, kvw:])
    outs = []
    for g in range(N_KV_HEADS):
        qs = _stack_heads(q, g).astype(BF16)
        kh = kv[:, g * HEAD_DIM:(g + 1) * HEAD_DIM].astype(BF16)
        s = lax.dot_general(qs, kh, (((1,), (1,)), ((), ())), preferred_element_type=F32)
        outs.append(_softmax_pv(s, sink_ref[g][:, :1], v_ones, g))
    o_ref[...] = _unstack_heads(outs, nq).astype(BF16)


def _context_attention(pc, sink, batch, n_ctx):
    grp = N_Q_HEADS // N_KV_HEADS
    qw = N_Q_HEADS * HEAD_DIM
    kvw2 = 2 * N_KV_HEADS * HEAD_DIM
    return pl.pallas_call(
        _cattn_kernel,
        grid=(batch,),
        in_specs=[pl.BlockSpec((n_ctx, qw), lambda b: (b, COL_Q // qw)),
                  pl.BlockSpec((n_ctx, kvw2), lambda b: (b, COL_KV // kvw2)),
                  pl.BlockSpec((N_KV_HEADS, grp * n_ctx, LANES), lambda b: (0, 0, 0))],
        out_specs=pl.BlockSpec((n_ctx, qw), lambda b: (b, 0)),
        out_shape=jax.ShapeDtypeStruct((batch * n_ctx, qw), BF16),
        compiler_params=_cparams(("parallel",), 40),
        name="context_attention",
    )(pc, pc, _sink_cols(sink, n_ctx))


FFT_TILE = 8


def _fft1_kernel(v_ref, w_ref, tc_ref, ts_ref, o_ref):
    df = D_FOURIER
    w = w_ref[...]
    n2 = w.shape[0] // 2
    for i in range(FFT_TILE):
        x = v_ref[:, i, :]
        z = jnp.concatenate([x[:, :df], x[:, df:]], axis=0).astype(BF16)
        c = _dot(w, z)
        cr, ci = c[:n2], c[n2:]
        tc = jnp.concatenate([tc_ref[i]] * (df // LANES), axis=1)
        ts = jnp.concatenate([ts_ref[i]] * (df // LANES), axis=1)
        o_ref[i, :, :df] = (cr * tc - ci * ts).astype(BF16)
        o_ref[i, :, df:] = (cr * ts + ci * tc).astype(BF16)


def _fft2_kernel(y_ref, w_ref, o_ref):
    df = D_FOURIER
    w = w_ref[...]
    for i in range(FFT_TILE):
        y = jnp.concatenate([y_ref[:, 2 * i * df:(2 * i + 1) * df],
                             y_ref[:, (2 * i + 1) * df:(2 * i + 2) * df]], axis=0)
        o_ref[:, i, :] = _dot(w, y)


def _fft_split(s):
    n1 = 1 << ((s.bit_length() - 1) // 2)
    return n1, s // n1


def _fft_tables(s):
    n1, n2 = _fft_split(s)
    c2, s2 = _dft_tables(n2)
    w1 = jnp.concatenate([jnp.concatenate([c2, -s2], axis=1),
                          jnp.concatenate([s2, c2], axis=1)], axis=0).astype(BF16)
    c1, s1 = _dft_tables(n1)
    w2 = jnp.concatenate([c1, -s1], axis=1).astype(BF16)
    ang = (jnp.arange(n1)[:, None] * jnp.arange(n2)[None, :]).astype(F32) * (2.0 * math.pi / s)
    tc = jnp.broadcast_to(jnp.cos(ang)[:, :, None], (n1, n2, LANES))
    ts = jnp.broadcast_to(jnp.sin(ang)[:, :, None], (n1, n2, LANES))
    return w1, w2, tc, ts


def _position_dft(v, tables, batch, s):
    w1, w2, tc, ts = tables
    n1, n2 = _fft_split(s)
    df2 = 2 * D_FOURIER
    nt1 = n1 // FFT_TILE
    nt2 = n2 // FFT_TILE
    stage1 = pl.pallas_call(
        _fft1_kernel,
        grid=(batch, nt1),
        in_specs=[pl.BlockSpec((n2, FFT_TILE, df2), lambda b, j: (b, j, 0)),
                  pl.BlockSpec(w1.shape, lambda b, j: (0, 0)),
                  pl.BlockSpec((FFT_TILE, n2, LANES), lambda b, j: (j, 0, 0)),
                  pl.BlockSpec((FFT_TILE, n2, LANES), lambda b, j: (j, 0, 0))],
        out_specs=pl.BlockSpec((FFT_TILE, n2, df2), lambda b, j: (b * nt1 + j, 0, 0)),
        out_shape=jax.ShapeDtypeStruct((batch * n1, n2, df2), BF16),
        compiler_params=_cparams(("parallel", "parallel"), 40),
        name="fft_stage1",
    )(v.reshape(batch * n2, n1, df2), w1, tc, ts)
    out = pl.pallas_call(
        _fft2_kernel,
        grid=(batch, nt2),
        in_specs=[pl.BlockSpec((n1, FFT_TILE * df2), lambda b, j: (b, j)),
                  pl.BlockSpec(w2.shape, lambda b, j: (0, 0))],
        out_specs=pl.BlockSpec((n1, FFT_TILE, D_FOURIER), lambda b, j: (b, j, 0)),
        out_shape=jax.ShapeDtypeStruct((batch * n1, n2, D_FOURIER), F32),
        compiler_params=_cparams(("parallel", "parallel"), 40),
        name="fft_stage2",
    )(stage1.reshape(batch * n1, n2 * df2), w2)
    return out.reshape(batch * s, D_FOURIER)


def _dft_tables(n):
    k = jnp.arange(n, dtype=I32)
    ang = ((k[:, None] * k[None, :]) % n).astype(F32) * (2.0 * math.pi / n)
    scale = n ** -0.5
    return jnp.cos(ang) * scale, jnp.sin(ang) * scale


def _channel_dft_matrix():
    cg = D_FOURIER // N_FOURIER_GROUPS
    c, s = _dft_tables(cg)
    eye = jnp.eye(N_FOURIER_GROUPS, dtype=F32)
    return jnp.concatenate([jnp.kron(eye, c), jnp.kron(eye, s)], axis=1).astype(BF16)


def _merge_kernel(ba_ref, bb_ref, bc_ref, bd_ref, gl_ref, gb_ref, wa_ref, wb_ref, wc_ref, wd_ref,
                  wo_ref, xs_ref, m2_ref, m3_ref, m4_ref, pg_ref, fg_ref, rw_ref,
                  xo_ref, hp_ref, aff_ref, aff_t_ref):
    tm, d = xs_ref.shape
    halves = [pl.ds(i * (tm // 2), tm // 2) for i in range(2)]
    ys = []
    for rs in halves:
        y = None
        for i, (b_ref, w_ref) in enumerate(((ba_ref, wa_ref), (bb_ref, wb_ref),
                                            (bc_ref, wc_ref), (bd_ref, wd_ref))):
            gate = _tanh_gate(gl_ref[rs, i * d:(i + 1) * d].astype(F32) + gb_ref[:, i * d:(i + 1) * d])
            term = gate * _dot(b_ref[rs, :].astype(BF16), w_ref[...])
            y = term if y is None else y + term
        ys.append(y.astype(BF16))
    zs = [_dot(y, wo_ref[...]) for y in ys]
    hs = []
    for rs, z in zip(halves, zs):
        xs = xs_ref[rs, :] + m2_ref[0] * _rms(z, pg_ref[...])
        xo_ref[rs, :] = xs
        h = _rms(xs, fg_ref[...]) * (1.0 + m4_ref[0]) + m3_ref[0]
        hp_ref[rs, :] = h.astype(BF16).astype(F32)
        hs.append(h)
    for rs, h in zip(halves, hs):
        logits = _dot3(h, rw_ref[...])
        lane = lax.broadcasted_iota(I32, logits.shape, 1)
        logits = jnp.where(lane < N_EXPERTS, logits, NEG_INF)
        e = jnp.exp(logits - jnp.max(logits, axis=1, keepdims=True))
        aff = e / jnp.sum(e, axis=1, keepdims=True)
        aff_ref[rs, :] = aff
        aff_t_ref[:, rs] = aff.T[:N_EXPERTS]


def _merge(ba, bb, bc, bd, p, gate_b, wa, wb, wc, wd, wo, xs, m2, m3, m4, post_g, ffn_g, rw,
           rows_per_group):
    r, d = xs.shape
    tm = min(512, rows_per_group)
    tiles_per_group = rows_per_group // tm
    gw = N_BRANCH * d

    def rows(width):
        return pl.BlockSpec((tm, width), lambda i: (i, 0))

    def const(shape):
        return pl.BlockSpec(shape, lambda i: (0,) * len(shape))

    mod = pl.BlockSpec((1, 1, d), lambda i: (i // tiles_per_group, 0, 0))
    half = d // 2
    return pl.pallas_call(
        _merge_kernel,
        grid=(r // tm,),
        in_specs=[rows(half), rows(half), rows(half), rows(half),
                  pl.BlockSpec((tm, gw), lambda i: (i, COL_G // gw)),
                  const((1, gw)),
                  const((half, d)), const((half, d)), const((half, d)), const((half, d)),
                  const((d, d)),
                  rows(d), mod, mod, mod, const((1, d)), const((1, d)), const((d, LANES))],
        out_specs=[rows(d), rows(d), rows(LANES),
                   pl.BlockSpec((N_EXPERTS, tm), lambda i: (0, i))],
        out_shape=[jax.ShapeDtypeStruct((r, d), F32),
                   jax.ShapeDtypeStruct((r, d), F32),
                   jax.ShapeDtypeStruct((r, LANES), F32),
                   jax.ShapeDtypeStruct((N_EXPERTS, r), F32)],
        compiler_params=_cparams(("parallel",), 48),
        name="merge_router",
    )(ba, bb, bc, bd, p, 0.5 * gate_b.reshape(1, gw), wa, wb, wc, wd, wo, xs, m2, m3, m4,
      post_g.reshape(1, d), ffn_g.reshape(1, d), rw)


def _route_kernel(aff_ref, aff_t_ref, tri_ref, tl_ref, idx_ref, val_ref, *, cap):
    s = aff_ref.shape[0]
    nslot = idx_ref.shape[-1]
    aff = aff_ref[...]

    def as_float(bits):
        return lax.bitcast_convert_type(bits, F32)

    def count(mask):
        part = jnp.sum(jnp.where(mask, 1.0, 0.0).reshape(s // 64, 64, LANES), axis=0)
        return jnp.sum(part, axis=0, keepdims=True)

    def search(i, thr):
        cand = thr | lax.shift_left(jnp.int32(1), 30 - i)
        return jnp.where(count(aff >= as_float(cand)) >= cap, cand, thr)

    thr = lax.fori_loop(0, 31, search, jnp.zeros((1, LANES), I32))
    gt = aff >= as_float(jnp.maximum(thr + 1, MIN_NORMAL_BITS))
    eq = (aff >= as_float(thr)) & jnp.logical_not(gt)
    need = cap - count(gt)

    tri = tri_ref[...]

    def cumsum_excl(m):
        off = jnp.zeros((1, LANES), F32)
        outs = []
        for c in range(s // LANES):
            mc = m[c * LANES:(c + 1) * LANES]
            cs = _dot(tri, mc.astype(BF16))
            outs.append(cs - mc + off)
            off = off + cs[LANES - 1:LANES, :]
        return jnp.concatenate(outs, axis=0)

    eq_f = jnp.where(eq, 1.0, 0.0)
    sel = gt | (eq & (cumsum_excl(eq_f) < need))
    sel_f = jnp.where(sel, 1.0, 0.0)
    pos = jnp.where(sel, cumsum_excl(sel_f), -1.0)

    slot = lax.broadcasted_iota(I32, (s, nslot), 1).astype(F32)
    tl = tl_ref[...]
    row = lax.broadcasted_iota(I32, tl.shape, 0)
    vals = []
    for e in range(N_EXPERTS):
        onehot = jnp.where(pos[:, e:e + 1] == slot, 1.0, 0.0).astype(BF16)
        a = aff_t_ref[e:e + 1, :]
        a_hi = a.astype(BF16).astype(F32)
        a_mid = (a - a_hi).astype(BF16).astype(F32)
        a_lo = a - a_hi - a_mid
        lhs = jnp.where(row == 2, a_hi, jnp.where(row == 3, a_mid, jnp.where(row == 4, a_lo, tl)))
        res = _dot(lhs.astype(BF16), onehot)
        idx_ref[0, e:e + 1, :] = (res[0:1] * 64.0 + res[1:2] + 0.5).astype(I32)
        vals.append(res[2:3] + res[3:4] + res[4:5])
    vals = jnp.concatenate(vals + [jnp.zeros((LANES - N_EXPERTS, nslot), F32)], axis=0)
    vals_t = jnp.concatenate([vals[:, c * LANES:(c + 1) * LANES].T for c in range(nslot // LANES)],
                             axis=0)
    val_ref[...] = vals_t[:cap]


def _route(aff, aff_t, batch, s, cap):
    nslot = max(cap, LANES)
    tri = (jnp.arange(LANES)[:, None] >= jnp.arange(LANES)[None, :]).astype(BF16)
    t = jnp.arange(s)
    tl = jnp.zeros((8, s), F32).at[0].set(t // 64).at[1].set(t % 64)
    idx, vals = pl.pallas_call(
        functools.partial(_route_kernel, cap=cap),
        grid=(batch,),
        in_specs=[pl.BlockSpec((s, LANES), lambda b: (b, 0)),
                  pl.BlockSpec((N_EXPERTS, s), lambda b: (0, b)),
                  pl.BlockSpec((LANES, LANES), lambda b: (0, 0)),
                  pl.BlockSpec((8, s), lambda b: (0, 0))],
        out_specs=[pl.BlockSpec((1, N_EXPERTS, nslot), lambda b: (b, 0, 0)),
                   pl.BlockSpec((cap, LANES), lambda b: (b, 0))],
        out_shape=[jax.ShapeDtypeStruct((batch, N_EXPERTS, nslot), I32),
                   jax.ShapeDtypeStruct((batch * cap, LANES), F32)],
        compiler_params=_cparams(("parallel",), 48),
        name="route_topk",
    )(aff, aff_t, tri, tl)
    return idx[:, :, :cap].reshape(-1), vals


def _tile_row(t):
    return lax.shift_right_logical(t, SUBLANES.bit_length() - 1), t & (SUBLANES - 1)


def _gather_kernel(idx_ref, h_ref, xg_ref, g_ref, *, cap):
    b = pl.program_id(0)
    e = pl.program_id(1)
    base = (b * N_EXPERTS + e) * cap

    def body(jg, carry):
        j0 = pl.multiple_of(jg * SUBLANES, SUBLANES)
        for k in range(SUBLANES):
            t = idx_ref[base + j0 + k]
            hi, lo = _tile_row(t)
            g_ref[jg, pl.ds(k, 1), :] = h_ref[hi, pl.ds(lo, 1), :]
        return carry

    lax.fori_loop(0, cap // SUBLANES, body, 0)
    xg_ref[...] = g_ref[...].reshape(xg_ref.shape).astype(BF16)


def _gather(idx, hp, batch, s, cap):
    d = hp.shape[1]
    grid_spec = pltpu.PrefetchScalarGridSpec(
        num_scalar_prefetch=1,
        grid=(batch, N_EXPERTS),
        in_specs=[pl.BlockSpec((s // SUBLANES, SUBLANES, d), lambda b, e, idx: (b, 0, 0))],
        out_specs=pl.BlockSpec((cap, d), lambda b, e, idx: (e * batch + b, 0)),
        scratch_shapes=[pltpu.VMEM((cap // SUBLANES, SUBLANES, d), F32)])
    return pl.pallas_call(
        functools.partial(_gather_kernel, cap=cap),
        grid_spec=grid_spec,
        out_shape=jax.ShapeDtypeStruct((N_EXPERTS * batch * cap, d), BF16),
        compiler_params=_cparams(("arbitrary", "arbitrary"), 48),
        name="moe_gather",
    )(idx, hp.reshape(-1, SUBLANES, d))


def _ffn_kernel(x_ref, v_ref, w1_ref, w3_ref, w2_ref, y_ref, w1b, w3b, w2b):
    @pl.when(pl.program_id(1) == 0)
    def _():
        w1b[...] = w1_ref[0, 0].astype(BF16)
        w3b[...] = w3_ref[0, 0].astype(BF16)
        w2b[...] = w2_ref[0, 0].astype(BF16)

    x = x_ref[...]
    hid = _silu(_dot(x, w1b[...])) * _dot(x, w3b[...])
    y = _dot(hid.astype(BF16), w2b[...])
    lane = lax.broadcasted_iota(I32, v_ref.shape, 1)
    v = jnp.sum(jnp.where(lane == pl.program_id(0), v_ref[...], 0.0), axis=1, keepdims=True)
    y_ref[...] = y * v


def _ffn(xg, vals, w1, w3, w2, layer, rows_per_expert):
    rows, d = xg.shape
    f = w1.shape[-1]
    tm = min(512, rows_per_expert)
    nt = rows_per_expert // tm

    def wspec(a, c):
        return pl.BlockSpec((1, 1, a, c), lambda e, m: (layer, e, 0, 0))

    return pl.pallas_call(
        _ffn_kernel,
        grid=(N_EXPERTS, nt),
        in_specs=[pl.BlockSpec((tm, d), lambda e, m: (e * nt + m, 0)),
                  pl.BlockSpec((tm, LANES), lambda e, m: (m, 0)),
                  wspec(d, f), wspec(d, f), wspec(f, d)],
        out_specs=pl.BlockSpec((tm, d), lambda e, m: (e * nt + m, 0)),
        out_shape=jax.ShapeDtypeStruct((rows, d), F32),
        scratch_shapes=[pltpu.VMEM((d, f), BF16), pltpu.VMEM((d, f), BF16), pltpu.VMEM((f, d), BF16)],
        compiler_params=_cparams(("parallel", "arbitrary"), 56),
        name="expert_ffn",
    )(xg, vals, w1, w3, w2)


def _combine_kernel(idx_ref, y_ref, xs_ref, m5_ref, g_ref, o_ref, acc_ref, *, cap, tf):
    b = pl.program_id(0)
    step = pl.program_id(1)

    @pl.when(step == 0)
    def _():
        acc_ref[...] = jnp.zeros_like(acc_ref)

    @pl.when(step < N_EXPERTS)
    def _():
        base = (b * N_EXPERTS + step) * cap

        def body(jg, carry):
            j0 = pl.multiple_of(jg * SUBLANES, SUBLANES)
            toks = [_tile_row(idx_ref[base + j0 + k]) for k in range(SUBLANES)]
            rows = [acc_ref[hi, pl.ds(lo, 1), :] for hi, lo in toks]
            for k, (hi, lo) in enumerate(toks):
                acc_ref[hi, pl.ds(lo, 1), :] = rows[k] + y_ref[jg, pl.ds(k, 1), :]
            return carry

        lax.fori_loop(0, cap // SUBLANES, body, 0)

    @pl.when(step >= N_EXPERTS)
    def _():
        r0 = pl.multiple_of((step - N_EXPERTS) * (tf // SUBLANES), tf // SUBLANES)
        moe = acc_ref[pl.ds(r0, tf // SUBLANES)].reshape(o_ref.shape)
        o_ref[...] = xs_ref[...] + m5_ref[0] * _rms(moe, g_ref[...])


def _combine(idx, y, xs, m5, post_g, batch, s, cap, shared_mod):
    d = xs.shape[1]
    tf = min(512, s)
    nfin = s // tf

    def chunk_map(b, st, idx):
        return (b * nfin + jnp.maximum(st - N_EXPERTS, 0), 0)

    grid_spec = pltpu.PrefetchScalarGridSpec(
        num_scalar_prefetch=1,
        grid=(batch, N_EXPERTS + nfin),
        in_specs=[pl.BlockSpec((cap // SUBLANES, SUBLANES, d),
                               lambda b, st, idx: (jnp.minimum(st, N_EXPERTS - 1) * batch + b, 0, 0)),
                  pl.BlockSpec((tf, d), chunk_map),
                  pl.BlockSpec((1, 1, d), lambda b, st, idx: (0 if shared_mod else b, 0, 0)),
                  pl.BlockSpec((1, d), lambda b, st, idx: (0, 0))],
        out_specs=pl.BlockSpec((tf, d), chunk_map),
        scratch_shapes=[pltpu.VMEM((s // SUBLANES, SUBLANES, d), F32)])
    return pl.pallas_call(
        functools.partial(_combine_kernel, cap=cap, tf=tf),
        grid_spec=grid_spec,
        out_shape=jax.ShapeDtypeStruct(xs.shape, F32),
        compiler_params=_cparams(("arbitrary", "arbitrary"), 48),
        name="moe_combine",
    )(idx, y.reshape(-1, SUBLANES, d), xs, m5, post_g.reshape(1, d))


def _prep_w_in(w):
    d = w.shape[0]
    kv = 2 * N_KV_HEADS * HEAD_DIM
    o_q = 2 * D_CONV + 3 * D_SHORT
    o_k = o_q + N_Q_HEADS * HEAD_DIM
    o_f = o_k + kv
    o_g = o_f + D_FOURIER
    parts = [w[:, :o_k], w[:, o_f:o_g], w[:, o_k:o_f],
             jnp.zeros((d, COL_G - COL_KV - kv), w.dtype), 0.5 * w[:, o_g:]]
    return jnp.concatenate(parts, axis=1).astype(BF16)


def _rope_tables(s):
    t = jnp.arange(s)
    row = (t // GRID_W).astype(F32)
    col = (t % GRID_W).astype(F32)
    nf = HEAD_DIM // 4
    inv = ROPE_BASE ** (-jnp.arange(nf, dtype=F32) / nf)
    ar = row[:, None] * inv
    ac = col[:, None] * inv
    cos = jnp.concatenate([jnp.cos(ar), jnp.cos(ar), jnp.cos(ac), jnp.cos(ac)], axis=1)
    sin = jnp.concatenate([-jnp.sin(ar), jnp.sin(ar), -jnp.sin(ac), jnp.sin(ac)], axis=1)
    rep = LANES // HEAD_DIM
    return jnp.tile(cos, (1, rep)), jnp.tile(sin, (1, rep))


def _moe(routed, xs, m5, post_g, w1, w3, w2, layer, batch, s, shared_mod):
    hp, aff, aff_t = routed
    cap = CAPACITY_FACTOR * s // N_EXPERTS
    idx, vals = _route(aff, aff_t, batch, s, cap)
    xg = _gather(idx, hp, batch, s, cap)
    y = _ffn(xg, vals, w1, w3, w2, layer, batch * cap)
    return _combine(idx, y, xs, m5, post_g, batch, s, cap, shared_mod)


def kernel(x, c, ctx, c_ctx, ada_w, ada_b, pre_mix_g, post_mix_g, pre_ffn_g, post_ffn_g, w_in, gate_b, conv_a_w, conv_a_b, ln_a_g, ln_a_b, w_a_out, conv_b_w, w_b_out, sink, w_c_out, w_d_out, w_o, router_w, exp_w1, exp_w3, exp_w2):
    batch, s, d = x.shape
    n_ctx = ctx.shape[1]
    depth = ada_w.shape[0]

    cvec = jnp.zeros((16, d), F32).at[:batch].set(c).at[batch].set(c_ctx)
    mod = _ada(cvec, ada_w, ada_b)
    cos_t, sin_t = _rope_tables(s)
    bd = _channel_dft_matrix()
    dft_x = _fft_tables(s)
    dft_c = _fft_tables(n_ctx)

    xs = x.reshape(batch * s, d)
    cs = ctx.reshape(batch * n_ctx, d)
    for l in range(depth):
        last = l == depth - 1
        mx = [mod[l, :batch, k * d:(k + 1) * d].reshape(batch, 1, d) for k in range(6)]
        mc = [mod[l, batch:batch + 1, k * d:(k + 1) * d].reshape(1, 1, d) for k in range(6)]
        g_pre = pre_mix_g[l].reshape(1, d)
        w = _prep_w_in(w_in[l])
        wa, wb, wc, wd = ((0.5 * t[l]).astype(BF16) for t in (w_a_out, w_b_out, w_c_out, w_d_out))
        wo = w_o[l].astype(BF16)
        rw = jnp.zeros((d, LANES), F32).at[:, :N_EXPERTS].set(router_w[l])

        p = _inproj(xs, mx[0], mx[1], g_pre, w, s)
        pc = _inproj(cs, mc[0], mc[1], g_pre, w, batch * n_ctx)

        def mixer(pp, att, seq, tables, xres, m, rows_per_group):
            ba = _conformer(pp, conv_a_w[l], conv_a_b[l], ln_a_g[l], ln_a_b[l], batch, seq)
            bb = _short_conv(pp, conv_b_w[l], batch, seq)
            v = _mm(pp, bd, a_cols=COL_F, out_dtype=F32)
            bf = _position_dft(v, tables, batch, seq)
            return _merge(ba, bb, att, bf, pp, gate_b[l], wa, wb, wc, wd, wo, xres,
                          m[2], m[3], m[4], post_mix_g[l], pre_ffn_g[l], rw, rows_per_group)

        att_x = _latent_attention(p, pc, cos_t, sin_t, sink[l], batch, s, n_ctx)
        xs, *routed = mixer(p, att_x, s, dft_x, xs, mx, s)
        xs = _moe(routed, xs, mx[5], post_ffn_g[l], exp_w1, exp_w3, exp_w2, l, batch, s, False)
        if not last:
            att_c = _context_attention(pc, sink[l], batch, n_ctx)
            cs, *routed_c = mixer(pc, att_c, n_ctx, dft_c, cs, mc, batch * n_ctx)
            cs = _moe(routed_c, cs, mc[5], post_ffn_g[l], exp_w1, exp_w3, exp_w2, l, batch, n_ctx, True)
    return xs.reshape(batch, s, d)
```

```python
import functools
import math

import jax
import jax.numpy as jnp
from jax import lax
from jax.experimental import pallas as pl
from jax.experimental.pallas import tpu as pltpu

F32 = jnp.float32
BF16 = jnp.bfloat16
I32 = jnp.int32

D_MODEL = 1024
GRID_W = 64
D_CONV = 512
CONV_K = 31
D_SHORT = 512
SHORT_K = 3
N_Q_HEADS = 8
N_KV_HEADS = 2
HEAD_DIM = 64
BLOCK = 128
D_FOURIER = 512
N_FOURIER_GROUPS = 4
N_BRANCH = 4
N_EXPERTS = 16
CAPACITY_FACTOR = 2
D_EXPERT = 1024
ROPE_BASE = 10000.0
EPS = 1e-6
NEG_INF = -1e30
LOG2E = 1.4426950408889634
MIN_NORMAL_BITS = 0x00800000

LANES = 128
SUBLANES = 8
HALO = 16
CONV_CHUNK = 64

COL_A = 0
COL_SHORT = 1024
COL_Q = 2560
COL_F = 3072
COL_KV = 3584
COL_G = 4096
N_PROJ = 8192
INPROJ_TN = 1024


def _cparams(sem, vmem_mb):
    return pltpu.CompilerParams(dimension_semantics=sem,
                                vmem_limit_bytes=vmem_mb * 1024 * 1024)


def _sigmoid(x):
    return 0.5 * jnp.tanh(0.5 * x) + 0.5


def _tanh_gate(half_x):
    return jnp.tanh(half_x) + 1.0


def _silu(x):
    return x * _sigmoid(x)


def _rms(x, g):
    return x * lax.rsqrt(jnp.mean(x * x, axis=-1, keepdims=True) + EPS) * g


def _split_bf16(x):
    hi = x.astype(BF16)
    lo = (x - hi.astype(F32)).astype(BF16)
    return hi, lo


def _dot(a, b):
    return jnp.dot(a, b, preferred_element_type=F32)


def _dot3(a, b):
    ah, al = _split_bf16(a)
    bh, bl = _split_bf16(b)
    return _dot(ah, bh) + _dot(ah, bl) + _dot(al, bh)


def _ada_kernel(c_ref, w_ref, b_ref, o_ref):
    c = c_ref[...]
    o_ref[0] = _dot3(_silu(c), w_ref[0]) + b_ref[0]


def _ada(cvec, ada_w, ada_b):
    nl, d, n6 = ada_w.shape
    rows = cvec.shape[0]
    return pl.pallas_call(
        _ada_kernel,
        grid=(nl, n6 // d),
        in_specs=[pl.BlockSpec((rows, d), lambda l, j: (0, 0)),
                  pl.BlockSpec((1, d, d), lambda l, j: (l, 0, j)),
                  pl.BlockSpec((1, 1, d), lambda l, j: (l, 0, j))],
        out_specs=pl.BlockSpec((1, rows, d), lambda l, j: (l, 0, j)),
        out_shape=jax.ShapeDtypeStruct((nl, rows, n6), F32),
        compiler_params=_cparams(("parallel", "parallel"), 40),
        name="ada_mod",
    )(cvec, ada_w, ada_b.reshape(nl, 1, n6))


def _inproj_kernel(x_ref, sh_ref, sc_ref, g_ref, w_ref, o_ref, h_ref):
    @pl.when(pl.program_id(1) == 0)
    def _():
        y = _rms(x_ref[...], g_ref[...])
        h_ref[...] = (y * (1.0 + sc_ref[0]) + sh_ref[0]).astype(BF16)

    o_ref[...] = _dot(h_ref[...], w_ref[...]).astype(BF16)


def _inproj(x2d, shift, scale, gain, w, rows_per_group):
    r, d = x2d.shape
    n = w.shape[1]
    tm = min(2048, rows_per_group)
    tn = INPROJ_TN
    tiles_per_group = rows_per_group // tm
    mod_spec = pl.BlockSpec((1, 1, d), lambda i, j: (i // tiles_per_group, 0, 0))
    return pl.pallas_call(
        _inproj_kernel,
        grid=(r // tm, n // tn),
        in_specs=[pl.BlockSpec((tm, d), lambda i, j: (i, 0)),
                  mod_spec, mod_spec,
                  pl.BlockSpec((1, d), lambda i, j: (0, 0)),
                  pl.BlockSpec((d, tn), lambda i, j: (0, j))],
        out_specs=pl.BlockSpec((tm, tn), lambda i, j: (i, j)),
        out_shape=jax.ShapeDtypeStruct((r, n), BF16),
        scratch_shapes=[pltpu.VMEM((tm, d), BF16)],
        compiler_params=_cparams(("parallel", "arbitrary"), 48),
        name="inproj",
    )(x2d, shift, scale, gain, w)


def _mm_kernel(a_ref, b_ref, o_ref):
    o_ref[...] = _dot(a_ref[...], b_ref[...]).astype(o_ref.dtype)


def _mm(a, b, *, a_cols=None, out_dtype=BF16, tm=1024):
    k, n = b.shape
    r = a.shape[0]
    cb = 0 if a_cols is None else a_cols // k
    tm = min(tm, r)
    return pl.pallas_call(
        _mm_kernel,
        grid=(r // tm,),
        in_specs=[pl.BlockSpec((tm, k), lambda i: (i, cb)),
                  pl.BlockSpec((k, n), lambda i: (0, 0))],
        out_specs=pl.BlockSpec((tm, n), lambda i: (i, 0)),
        out_shape=jax.ShapeDtypeStruct((r, n), out_dtype),
        compiler_params=_cparams(("parallel",), 40),
        name="matmul",
    )(a, b)


def _fill_window(win_ref, cur, prev, nxt, t):
    n = pl.program_id(1)
    last = pl.num_programs(1) - 1
    win_ref[HALO:HALO + t, :] = cur
    win_ref[0:HALO, :] = jnp.where(n > 0, prev, 0.0)
    win_ref[HALO + t:HALO + t + HALO, :] = jnp.where(n < last, nxt, 0.0)


def _dwconv_chunk(win_ref, t0, w_ref, ktaps):
    off = HALO - ktaps // 2
    nfull = -(-(off + ktaps) // 8) * 8
    w = win_ref[pl.ds(t0, CONV_CHUNK + nfull), :]
    acc = None
    for r in range(8):
        part = None
        for a in range(nfull // 8):
            j = 8 * a + r - off
            if 0 <= j < ktaps:
                term = w[8 * a:8 * a + CONV_CHUNK + 8] * w_ref[j:j + 1, :]
                part = term if part is None else part + term
        if part is not None:
            shifted = part[r:r + CONV_CHUNK]
            acc = shifted if acc is None else acc + shifted
    return acc


def _conformer_kernel(cur_ref, prev_ref, next_ref, cw_ref, cb_ref, lg_ref, lb_ref,
                      o_ref, win_ref, *, t):
    dc = o_ref.shape[-1]

    def glu(ref):
        blk = ref[...].astype(F32)
        return blk[:, :dc] * _sigmoid(blk[:, dc:])

    _fill_window(win_ref, glu(cur_ref), glu(prev_ref), glu(next_ref), t)

    def chunk(i, carry):
        t0 = pl.multiple_of(i * CONV_CHUNK, CONV_CHUNK)
        h = _dwconv_chunk(win_ref, t0, cw_ref, CONV_K) + cb_ref[...]
        mu = jnp.mean(h, axis=-1, keepdims=True)
        hc = h - mu
        y = hc * lax.rsqrt(jnp.mean(hc * hc, axis=-1, keepdims=True) + EPS)
        y = y * lg_ref[...] + lb_ref[...]
        o_ref[pl.ds(t0, CONV_CHUNK), :] = _silu(y).astype(BF16)
        return carry

    lax.fori_loop(0, t // CONV_CHUNK, chunk, 0)


def _halo_specs(width, col_block, s, t):
    nblk = t // HALO
    per_sample = s // HALO

    def prev_map(b, n):
        return (jnp.maximum(b * per_sample + n * nblk - 1, 0), col_block)

    def next_map(b, n):
        return (jnp.minimum(b * per_sample + (n + 1) * nblk, (b + 1) * per_sample - 1), col_block)

    return pl.BlockSpec((HALO, width), prev_map), pl.BlockSpec((HALO, width), next_map)


def _conformer(p, conv_w, conv_b, ln_g, ln_b, batch, s):
    t = min(512, s)
    nt = s // t
    prev_spec, next_spec = _halo_specs(2 * D_CONV, COL_A // (2 * D_CONV), s, t)
    vec = pl.BlockSpec((1, D_CONV), lambda b, n: (0, 0))
    return pl.pallas_call(
        functools.partial(_conformer_kernel, t=t),
        grid=(batch, nt),
        in_specs=[pl.BlockSpec((t, 2 * D_CONV), lambda b, n: (b * nt + n, COL_A // (2 * D_CONV))),
                  prev_spec, next_spec,
                  pl.BlockSpec((CONV_K, D_CONV), lambda b, n: (0, 0)),
                  vec, vec, vec],
        out_specs=pl.BlockSpec((t, D_CONV), lambda b, n: (b * nt + n, 0)),
        out_shape=jax.ShapeDtypeStruct((batch * s, D_CONV), BF16),
        scratch_shapes=[pltpu.VMEM((t + 2 * HALO, D_CONV), F32)],
        compiler_params=_cparams(("parallel", "parallel"), 40),
        name="conformer_conv",
    )(p, p, p, conv_w, conv_b.reshape(1, -1), ln_g.reshape(1, -1), ln_b.reshape(1, -1))


def _short_kernel(bg_ref, cg_ref, hv_ref, cgp_ref, hvp_ref, cgn_ref, hvn_ref, w_ref,
                  o_ref, win_ref, *, t):
    def prod(a_ref, b_ref):
        return a_ref[...].astype(F32) * b_ref[...].astype(F32)

    _fill_window(win_ref, prod(cg_ref, hv_ref), prod(cgp_ref, hvp_ref), prod(cgn_ref, hvn_ref), t)

    def chunk(i, carry):
        t0 = pl.multiple_of(i * CONV_CHUNK, CONV_CHUNK)
        h = _dwconv_chunk(win_ref, t0, w_ref, SHORT_K)
        bg = bg_ref[pl.ds(t0, CONV_CHUNK), :].astype(F32)
        o_ref[pl.ds(t0, CONV_CHUNK), :] = (bg * h).astype(BF16)
        return carry

    lax.fori_loop(0, t // CONV_CHUNK, chunk, 0)


def _short_conv(p, conv_w, batch, s):
    t = min(512, s)
    nt = s // t
    cb = COL_SHORT // D_SHORT
    cgp, cgn = _halo_specs(D_SHORT, cb + 1, s, t)
    hvp, hvn = _halo_specs(D_SHORT, cb + 2, s, t)

    def cur(k):
        return pl.BlockSpec((t, D_SHORT), lambda b, n: (b * nt + n, cb + k))

    return pl.pallas_call(
        functools.partial(_short_kernel, t=t),
        grid=(batch, nt),
        in_specs=[cur(0), cur(1), cur(2), cgp, hvp, cgn, hvn,
                  pl.BlockSpec((SHORT_K, D_SHORT), lambda b, n: (0, 0))],
        out_specs=pl.BlockSpec((t, D_SHORT), lambda b, n: (b * nt + n, 0)),
        out_shape=jax.ShapeDtypeStruct((batch * s, D_SHORT), BF16),
        scratch_shapes=[pltpu.VMEM((t + 2 * HALO, D_SHORT), F32)],
        compiler_params=_cparams(("parallel", "parallel"), 40),
        name="short_conv",
    )(p, p, p, p, p, p, p, conv_w)


def _rope(x, cos, sin):
    w = x.shape[1]
    lane = lax.broadcasted_iota(I32, x.shape, 1)
    swapped = jnp.where((lane & 31) < 16, pltpu.roll(x, w - 16, 1), pltpu.roll(x, 16, 1))
    return x * cos + swapped * sin


def _stack_heads(q, g):
    grp = N_Q_HEADS // N_KV_HEADS
    return jnp.concatenate(
        [q[:, (grp * g + i) * HEAD_DIM:(grp * g + i + 1) * HEAD_DIM] for i in range(grp)], axis=0)


def _softmax_pv(s, sink_col, v_ones, g):
    m = jnp.maximum(jnp.max(s, axis=1, keepdims=True), sink_col)
    o = _dot(jnp.exp2(s - m).astype(BF16), v_ones)
    den = o[:, LANES:LANES + 1] + jnp.exp2(sink_col - m)
    return o[:, g * HEAD_DIM:(g + 1) * HEAD_DIM] / den


def _with_ones(v):
    return jnp.concatenate([v, jnp.ones_like(v)], axis=1).astype(BF16)


def _unstack_heads(outs, nq):
    grp = N_Q_HEADS // N_KV_HEADS
    pieces = [o[i * nq:(i + 1) * nq] for o in outs for i in range(grp)]
    return jnp.concatenate(pieces, axis=1)


def _attn_kernel(q_ref, kv_ref, ckv_ref, cos_ref, sin_ref, bias_ref, sink_ref, o_ref, *, nb, qb):
    first = pl.program_id(1) * qb
    kvw = N_KV_HEADS * HEAD_DIM
    grp = N_Q_HEADS // N_KV_HEADS
    rep = N_Q_HEADS * HEAD_DIM // LANES

    q0 = pl.multiple_of(first * BLOCK, BLOCK)
    cq = cos_ref[pl.ds(q0, qb * BLOCK), :]
    sq = sin_ref[pl.ds(q0, qb * BLOCK), :]
    q = _rope(q_ref[...].astype(F32), jnp.concatenate([cq] * rep, axis=1),
              jnp.concatenate([sq] * rep, axis=1)) * (HEAD_DIM ** -0.5 * LOG2E)

    def kblock(j):
        start = pl.multiple_of(jnp.clip(first + j, 0, nb - 1) * BLOCK, BLOCK)
        kvb = kv_ref[pl.ds(start, BLOCK), :].astype(F32)
        k = _rope(kvb[:, :kvw], cos_ref[pl.ds(start, BLOCK), :], sin_ref[pl.ds(start, BLOCK), :])
        return k, kvb[:, kvw:].T

    blocks = [kblock(j) for j in range(-1, qb + 1)]
    ckv = ckv_ref[...].astype(F32)
    ctx_k = ckv[:, :kvw]
    ctx_vt = [ckv[i:i + BLOCK, kvw:].T for i in range(0, ckv.shape[0], BLOCK)]
    ones = jnp.ones((HEAD_DIM, 3 * BLOCK + ckv.shape[0]), F32)
    sinks = [sink_ref[g][:1, :] for g in range(N_KV_HEADS)]

    chains = [(i, g) for i in range(qb) for g in range(N_KV_HEADS)]
    scores = []
    for i, g in chains:
        bias_prev = bias_ref[:BLOCK, :] + jnp.where(first + i == 0, NEG_INF, 0.0)
        bias_next = bias_ref[BLOCK:, :] + jnp.where(first + i == nb - 1, NEG_INF, 0.0)
        qs = _stack_heads(q[i * BLOCK:(i + 1) * BLOCK], g).astype(BF16)
        k_all = jnp.concatenate([blocks[i][0], blocks[i + 1][0], blocks[i + 2][0], ctx_k], axis=0)
        kh = k_all[:, g * HEAD_DIM:(g + 1) * HEAD_DIM].astype(BF16)
        s = lax.dot_general(kh, qs, (((1,), (1,)), ((), ())), preferred_element_type=F32)
        scores.append(jnp.concatenate([s[:BLOCK] + bias_prev, s[BLOCK:2 * BLOCK],
                                       s[2 * BLOCK:3 * BLOCK] + bias_next, s[3 * BLOCK:]], axis=0))
    maxes = [jnp.maximum(jnp.max(s, axis=0, keepdims=True), sinks[g]) for s, (_, g) in zip(scores, chains)]
    exps = [jnp.exp2(s - m).astype(BF16) for s, m in zip(scores, maxes)]
    outs = [[] for _ in range(qb)]
    for (i, g), e, m in zip(chains, exps, maxes):
        v_t = jnp.concatenate([blocks[i][1], blocks[i + 1][1], blocks[i + 2][1]] + ctx_vt, axis=1)
        lhs = jnp.concatenate([v_t[g * HEAD_DIM:(g + 1) * HEAD_DIM], ones], axis=0).astype(BF16)
        o_t = _dot(lhs, e)
        den = o_t[HEAD_DIM:HEAD_DIM + 1] + jnp.exp2(sinks[g] - m)
        o_t = o_t * (1.0 / den)
        outs[i] += [o_t[:, h * BLOCK:(h + 1) * BLOCK].T[:, :HEAD_DIM] for h in range(grp)]
    for i in range(qb):
        o_ref[i * BLOCK:(i + 1) * BLOCK, :] = jnp.concatenate(outs[i], axis=1).astype(BF16)


def _band_bias():
    grp = N_Q_HEADS // N_KV_HEADS
    kj = jnp.arange(BLOCK)[:, None]
    qi = jnp.arange(BLOCK)[None, :]
    prev = jnp.where(qi <= kj, 0.0, NEG_INF)
    nxt = jnp.where(kj <= qi, 0.0, NEG_INF)
    return jnp.tile(jnp.concatenate([prev, nxt], axis=0).astype(F32), (1, grp))


def _sink_rows(sink):
    grp = N_Q_HEADS // N_KV_HEADS
    row = jnp.repeat(sink.astype(F32).reshape(N_KV_HEADS, grp) * LOG2E, BLOCK, axis=1)
    return jnp.broadcast_to(row[:, None, :], (N_KV_HEADS, 8, grp * BLOCK))


def _sink_cols(sink, nq):
    grp = N_Q_HEADS // N_KV_HEADS
    col = jnp.repeat(sink.astype(F32).reshape(N_KV_HEADS, grp) * LOG2E, nq, axis=1)
    return jnp.broadcast_to(col[:, :, None], (N_KV_HEADS, grp * nq, LANES))


def _latent_attention(p, pc, pc_kv_col, cos_t, sin_t, sink, batch, s, n_ctx):
    nb = s // BLOCK
    grp = N_Q_HEADS // N_KV_HEADS
    qw = N_Q_HEADS * HEAD_DIM
    kvw2 = 2 * N_KV_HEADS * HEAD_DIM
    bias = _band_bias()
    qb = 2 if nb % 2 == 0 else 1
    ns = nb // qb
    tab = pl.BlockSpec((s, LANES), lambda b, n: (0, 0))
    return pl.pallas_call(
        functools.partial(_attn_kernel, nb=nb, qb=qb),
        grid=(batch, ns),
        in_specs=[pl.BlockSpec((qb * BLOCK, qw), lambda b, n: (b * ns + n, COL_Q // qw)),
                  pl.BlockSpec((s, kvw2), lambda b, n: (b, COL_KV // kvw2)),
                  pl.BlockSpec((n_ctx, kvw2), lambda b, n: (b, pc_kv_col // kvw2)),
                  tab, tab,
                  pl.BlockSpec(bias.shape, lambda b, n: (0, 0)),
                  pl.BlockSpec((N_KV_HEADS, 8, grp * BLOCK), lambda b, n: (0, 0, 0))],
        out_specs=pl.BlockSpec((qb * BLOCK, qw), lambda b, n: (b * ns + n, 0)),
        out_shape=jax.ShapeDtypeStruct((batch * s, qw), BF16),
        compiler_params=_cparams(("parallel", "parallel"), 40),
        name="latent_attention",
    )(p, p, pc, cos_t, sin_t, bias, _sink_rows(sink))


def _cattn_kernel(q_ref, kv_ref, sink_ref, o_ref):
    nq = q_ref.shape[0]
    kvw = N_KV_HEADS * HEAD_DIM
    q = q_ref[...].astype(F32) * (HEAD_DIM ** -0.5 * LOG2E)
    kv = kv_ref[...].astype(F32)
    v_ones = _with_ones(kv[:, kvw:])
    outs = []
    for g in range(N_KV_HEADS):
        qs = _stack_heads(q, g).astype(BF16)
        kh = kv[:, g * HEAD_DIM:(g + 1) * HEAD_DIM].astype(BF16)
        s = lax.dot_general(qs, kh, (((1,), (1,)), ((), ())), preferred_element_type=F32)
        outs.append(_softmax_pv(s, sink_ref[g][:, :1], v_ones, g))
    o_ref[...] = _unstack_heads(outs, nq).astype(BF16)


def _context_attention(pc, sink, batch, n_ctx):
    grp = N_Q_HEADS // N_KV_HEADS
    qw = N_Q_HEADS * HEAD_DIM
    kvw2 = 2 * N_KV_HEADS * HEAD_DIM
    return pl.pallas_call(
        _cattn_kernel,
        grid=(batch,),
        in_specs=[pl.BlockSpec((n_ctx, qw), lambda b: (b, COL_Q // qw)),
                  pl.BlockSpec((n_ctx, kvw2), lambda b: (b, COL_KV // kvw2)),
                  pl.BlockSpec((N_KV_HEADS, grp * n_ctx, LANES), lambda b: (0, 0, 0))],
        out_specs=pl.BlockSpec((n_ctx, qw), lambda b: (b, 0)),
        out_shape=jax.ShapeDtypeStruct((batch * n_ctx, qw), BF16),
        compiler_params=_cparams(("parallel",), 40),
        name="context_attention",
    )(pc, pc, _sink_cols(sink, n_ctx))


FFT_TILE = 8


def _fft1_kernel(v_ref, w_ref, tc_ref, ts_ref, o_ref):
    df = D_FOURIER
    w = w_ref[...]
    n2 = w.shape[0] // 2
    for i in range(FFT_TILE):
        x = v_ref[:, i, :]
        z = jnp.concatenate([x[:, :df], x[:, df:]], axis=0).astype(BF16)
        c = _dot(w, z)
        cr, ci = c[:n2], c[n2:]
        tc = jnp.concatenate([tc_ref[i]] * (df // LANES), axis=1)
        ts = jnp.concatenate([ts_ref[i]] * (df // LANES), axis=1)
        o_ref[i, :, :df] = (cr * tc - ci * ts).astype(BF16)
        o_ref[i, :, df:] = (cr * ts + ci * tc).astype(BF16)


def _fft2_kernel(y_ref, w_ref, o_ref):
    df = D_FOURIER
    w = w_ref[...]
    for i in range(FFT_TILE):
        y = jnp.concatenate([y_ref[:, 2 * i * df:(2 * i + 1) * df],
                             y_ref[:, (2 * i + 1) * df:(2 * i + 2) * df]], axis=0)
        o_ref[:, i, :] = _dot(w, y)


def _fft_split(s):
    n1 = 1 << ((s.bit_length() - 1) // 2)
    return n1, s // n1


def _fft_tables(s):
    n1, n2 = _fft_split(s)
    c2, s2 = _dft_tables(n2)
    w1 = jnp.concatenate([jnp.concatenate([c2, -s2], axis=1),
                          jnp.concatenate([s2, c2], axis=1)], axis=0).astype(BF16)
    c1, s1 = _dft_tables(n1)
    w2 = jnp.concatenate([c1, -s1], axis=1).astype(BF16)
    ang = (jnp.arange(n1)[:, None] * jnp.arange(n2)[None, :]).astype(F32) * (2.0 * math.pi / s)
    tc = jnp.broadcast_to(jnp.cos(ang)[:, :, None], (n1, n2, LANES))
    ts = jnp.broadcast_to(jnp.sin(ang)[:, :, None], (n1, n2, LANES))
    return w1, w2, tc, ts


def _position_dft(v, tables, batch, s):
    w1, w2, tc, ts = tables
    n1, n2 = _fft_split(s)
    df2 = 2 * D_FOURIER
    nt1 = n1 // FFT_TILE
    nt2 = n2 // FFT_TILE
    stage1 = pl.pallas_call(
        _fft1_kernel,
        grid=(batch, nt1),
        in_specs=[pl.BlockSpec((n2, FFT_TILE, df2), lambda b, j: (b, j, 0)),
                  pl.BlockSpec(w1.shape, lambda b, j: (0, 0)),
                  pl.BlockSpec((FFT_TILE, n2, LANES), lambda b, j: (j, 0, 0)),
                  pl.BlockSpec((FFT_TILE, n2, LANES), lambda b, j: (j, 0, 0))],
        out_specs=pl.BlockSpec((FFT_TILE, n2, df2), lambda b, j: (b * nt1 + j, 0, 0)),
        out_shape=jax.ShapeDtypeStruct((batch * n1, n2, df2), BF16),
        compiler_params=_cparams(("parallel", "parallel"), 40),
        name="fft_stage1",
    )(v.reshape(batch * n2, n1, df2), w1, tc, ts)
    out = pl.pallas_call(
        _fft2_kernel,
        grid=(batch, nt2),
        in_specs=[pl.BlockSpec((n1, FFT_TILE * df2), lambda b, j: (b, j)),
                  pl.BlockSpec(w2.shape, lambda b, j: (0, 0))],
        out_specs=pl.BlockSpec((n1, FFT_TILE, D_FOURIER), lambda b, j: (b, j, 0)),
        out_shape=jax.ShapeDtypeStruct((batch * n1, n2, D_FOURIER), F32),
        compiler_params=_cparams(("parallel", "parallel"), 40),
        name="fft_stage2",
    )(stage1.reshape(batch * n1, n2 * df2), w2)
    return out.reshape(batch * s, D_FOURIER)


def _dft_tables(n):
    k = jnp.arange(n, dtype=I32)
    ang = ((k[:, None] * k[None, :]) % n).astype(F32) * (2.0 * math.pi / n)
    scale = n ** -0.5
    return jnp.cos(ang) * scale, jnp.sin(ang) * scale


def _channel_dft_matrix():
    cg = D_FOURIER // N_FOURIER_GROUPS
    c, s = _dft_tables(cg)
    eye = jnp.eye(N_FOURIER_GROUPS, dtype=F32)
    return jnp.concatenate([jnp.kron(eye, c), jnp.kron(eye, s)], axis=1).astype(BF16)


def _merge_kernel(ba_ref, bb_ref, bc_ref, bd_ref, gl_ref, gb_ref, wa_ref, wb_ref, wc_ref, wd_ref,
                  wo_ref, xs_ref, m2_ref, m3_ref, m4_ref, pg_ref, fg_ref, rw_ref,
                  xo_ref, hp_ref, aff_ref, aff_t_ref):
    tm, d = xs_ref.shape
    halves = [pl.ds(i * (tm // 2), tm // 2) for i in range(2)]
    ys = []
    for rs in halves:
        y = None
        for i, (b_ref, w_ref) in enumerate(((ba_ref, wa_ref), (bb_ref, wb_ref),
                                            (bc_ref, wc_ref), (bd_ref, wd_ref))):
            gate = _tanh_gate(gl_ref[rs, i * d:(i + 1) * d].astype(F32) + gb_ref[:, i * d:(i + 1) * d])
            term = gate * _dot(b_ref[rs, :].astype(BF16), w_ref[...])
            y = term if y is None else y + term
        ys.append(y.astype(BF16))
    zs = [_dot(y, wo_ref[...]) for y in ys]
    hs = []
    for rs, z in zip(halves, zs):
        xs = xs_ref[rs, :] + m2_ref[0] * _rms(z, pg_ref[...])
        xo_ref[rs, :] = xs
        h = _rms(xs, fg_ref[...]) * (1.0 + m4_ref[0]) + m3_ref[0]
        hp_ref[rs, :] = h.astype(BF16).astype(F32)
        hs.append(h)
    for rs, h in zip(halves, hs):
        logits = _dot3(h, rw_ref[...])
        lane = lax.broadcasted_iota(I32, logits.shape, 1)
        logits = jnp.where(lane < N_EXPERTS, logits, NEG_INF)
        e = jnp.exp(logits - jnp.max(logits, axis=1, keepdims=True))
        aff = e / jnp.sum(e, axis=1, keepdims=True)
        aff_ref[rs, :] = aff
        aff_t_ref[:, rs] = aff.T[:N_EXPERTS]


def _merge(ba, bb, bc, bd, p, gate_b, wa, wb, wc, wd, wo, xs, m2, m3, m4, post_g, ffn_g, rw,
           rows_per_group):
    r, d = xs.shape
    tm = min(512, rows_per_group)
    tiles_per_group = rows_per_group // tm
    gw = N_BRANCH * d

    def rows(width):
        return pl.BlockSpec((tm, width), lambda i: (i, 0))

    def const(shape):
        return pl.BlockSpec(shape, lambda i: (0,) * len(shape))

    mod = pl.BlockSpec((1, 1, d), lambda i: (i // tiles_per_group, 0, 0))
    half = d // 2
    return pl.pallas_call(
        _merge_kernel,
        grid=(r // tm,),
        in_specs=[rows(half), rows(half), rows(half), rows(half),
                  pl.BlockSpec((tm, gw), lambda i: (i, COL_G // gw)),
                  const((1, gw)),
                  const((half, d)), const((half, d)), const((half, d)), const((half, d)),
                  const((d, d)),
                  rows(d), mod, mod, mod, const((1, d)), const((1, d)), const((d, LANES))],
        out_specs=[rows(d), rows(d), rows(LANES),
                   pl.BlockSpec((N_EXPERTS, tm), lambda i: (0, i))],
        out_shape=[jax.ShapeDtypeStruct((r, d), F32),
                   jax.ShapeDtypeStruct((r, d), F32),
                   jax.ShapeDtypeStruct((r, LANES), F32),
                   jax.ShapeDtypeStruct((N_EXPERTS, r), F32)],
        compiler_params=_cparams(("parallel",), 48),
        name="merge_router",
    )(ba, bb, bc, bd, p, 0.5 * gate_b.reshape(1, gw), wa, wb, wc, wd, wo, xs, m2, m3, m4,
      post_g.reshape(1, d), ffn_g.reshape(1, d), rw)


def _route_kernel(aff_ref, aff_t_ref, tri_ref, tl_ref, idx_ref, val_ref, *, cap):
    s = aff_ref.shape[0]
    nslot = idx_ref.shape[-1]
    aff = aff_ref[...]

    def as_float(bits):
        return lax.bitcast_convert_type(bits, F32)

    def count(mask):
        part = jnp.sum(jnp.where(mask, 1.0, 0.0).reshape(s // 64, 64, LANES), axis=0)
        return jnp.sum(part, axis=0, keepdims=True)

    aff_t = aff_t_ref[...]

    def search(i, thr):
        cand = thr | lax.shift_left(jnp.int32(1), 30 - i)
        above = jnp.sum(jnp.where(aff_t >= as_float(cand), 1.0, 0.0), axis=1, keepdims=True)
        return jnp.where(above >= cap, cand, thr)

    thr = lax.fori_loop(0, 31, search, jnp.zeros((N_EXPERTS, 1), I32))

    def as_lane_row(col):
        block = jnp.concatenate([jnp.broadcast_to(col, (N_EXPERTS, LANES)),
                                 jnp.zeros((LANES - N_EXPERTS, LANES), F32)], axis=0)
        return block.T[0:1, :]

    gt = aff >= as_lane_row(as_float(jnp.maximum(thr + 1, MIN_NORMAL_BITS)))
    eq = (aff >= as_lane_row(as_float(thr))) & jnp.logical_not(gt)
    need = cap - count(gt)

    tri = tri_ref[...]

    def cumsum_excl(m):
        off = jnp.zeros((1, LANES), F32)
        outs = []
        for c in range(s // LANES):
            mc = m[c * LANES:(c + 1) * LANES]
            cs = _dot(tri, mc.astype(BF16))
            outs.append(cs - mc + off)
            off = off + cs[LANES - 1:LANES, :]
        return jnp.concatenate(outs, axis=0)

    eq_f = jnp.where(eq, 1.0, 0.0)
    sel = gt | (eq & (cumsum_excl(eq_f) < need))
    sel_f = jnp.where(sel, 1.0, 0.0)
    pos = jnp.where(sel, cumsum_excl(sel_f), -1.0)

    slot = lax.broadcasted_iota(I32, (s, nslot), 1).astype(F32)
    tl = tl_ref[...]
    row = lax.broadcasted_iota(I32, tl.shape, 0)
    vals = []
    for e in range(N_EXPERTS):
        onehot = jnp.where(pos[:, e:e + 1] == slot, 1.0, 0.0).astype(BF16)
        a = aff_t_ref[e:e + 1, :]
        a_hi = a.astype(BF16).astype(F32)
        a_mid = (a - a_hi).astype(BF16).astype(F32)
        a_lo = a - a_hi - a_mid
        lhs = jnp.where(row == 2, a_hi, jnp.where(row == 3, a_mid, jnp.where(row == 4, a_lo, tl)))
        res = _dot(lhs.astype(BF16), onehot)
        idx_ref[0, e:e + 1, :] = (res[0:1] * 64.0 + res[1:2] + 0.5).astype(I32)
        vals.append(res[2:3] + res[3:4] + res[4:5])
    vals = jnp.concatenate(vals + [jnp.zeros((LANES - N_EXPERTS, nslot), F32)], axis=0)
    vals_t = jnp.concatenate([vals[:, c * LANES:(c + 1) * LANES].T for c in range(nslot // LANES)],
                             axis=0)
    val_ref[...] = vals_t[:cap]


def _route(aff, aff_t, batch, s, cap):
    nslot = max(cap, LANES)
    tri = (jnp.arange(LANES)[:, None] >= jnp.arange(LANES)[None, :]).astype(BF16)
    t = jnp.arange(s)
    tl = jnp.zeros((8, s), F32).at[0].set(t // 64).at[1].set(t % 64)
    idx, vals = pl.pallas_call(
        functools.partial(_route_kernel, cap=cap),
        grid=(batch,),
        in_specs=[pl.BlockSpec((s, LANES), lambda b: (b, 0)),
                  pl.BlockSpec((N_EXPERTS, s), lambda b: (0, b)),
                  pl.BlockSpec((LANES, LANES), lambda b: (0, 0)),
                  pl.BlockSpec((8, s), lambda b: (0, 0))],
        out_specs=[pl.BlockSpec((1, N_EXPERTS, nslot), lambda b: (b, 0, 0)),
                   pl.BlockSpec((cap, LANES), lambda b: (b, 0))],
        out_shape=[jax.ShapeDtypeStruct((batch, N_EXPERTS, nslot), I32),
                   jax.ShapeDtypeStruct((batch * cap, LANES), F32)],
        compiler_params=_cparams(("parallel",), 48),
        name="route_topk",
    )(aff, aff_t, tri, tl)
    return idx[:, :, :cap].reshape(-1), vals


def _tile_row(t):
    return lax.shift_right_logical(t, SUBLANES.bit_length() - 1), t & (SUBLANES - 1)


def _experts_per_step(cap):
    return max(1, min(N_EXPERTS, 512 // cap))


def _gather_kernel(idx_ref, h_ref, xg_ref, g_ref, *, cap, eps):
    b = pl.program_id(0)
    first = pl.program_id(1) * eps
    groups = cap // SUBLANES

    for ei in range(eps):
        base = (b * N_EXPERTS + first + ei) * cap

        def body(jg, carry, ei=ei, base=base):
            j0 = pl.multiple_of(jg * SUBLANES, SUBLANES)
            for k in range(SUBLANES):
                t = idx_ref[base + j0 + k]
                hi, lo = _tile_row(t)
                g_ref[ei * groups + jg, pl.ds(k, 1), :] = h_ref[hi, pl.ds(lo, 1), :]
            return carry

        lax.fori_loop(0, groups, body, 0)
    xg_ref[...] = g_ref[...].reshape(xg_ref.shape).astype(BF16)


def _gather(idx, hp, batch, s, cap):
    d = hp.shape[1]
    eps = _experts_per_step(cap)
    grid_spec = pltpu.PrefetchScalarGridSpec(
        num_scalar_prefetch=1,
        grid=(batch, N_EXPERTS // eps),
        in_specs=[pl.BlockSpec((s // SUBLANES, SUBLANES, d), lambda b, e, idx: (b, 0, 0))],
        out_specs=pl.BlockSpec((eps, cap, d), lambda b, e, idx: (e, b, 0)),
        scratch_shapes=[pltpu.VMEM((eps * cap // SUBLANES, SUBLANES, d), F32)])
    xg = pl.pallas_call(
        functools.partial(_gather_kernel, cap=cap, eps=eps),
        grid_spec=grid_spec,
        out_shape=jax.ShapeDtypeStruct((N_EXPERTS, batch * cap, d), BF16),
        compiler_params=_cparams(("arbitrary", "arbitrary"), 48),
        name="moe_gather",
    )(idx, hp.reshape(-1, SUBLANES, d))
    return xg.reshape(N_EXPERTS * batch * cap, d)


def _ffn_kernel(*refs, n_sets):
    ins, rest = refs[:2 * n_sets], refs[2 * n_sets:]
    w1_ref, w3_ref, w2_ref = rest[:3]
    outs = rest[3:3 + n_sets]
    w1b, w3b, w2b = rest[3 + n_sets:]
    expert = pl.program_id(0)

    def run(x_ref, v_ref, y_ref):
        x = x_ref[...]
        hid = _silu(_dot(x, w1b[...])) * _dot(x, w3b[...])
        y = _dot(hid.astype(BF16), w2b[...])
        lane = lax.broadcasted_iota(I32, v_ref.shape, 1)
        v = jnp.sum(jnp.where(lane == expert, v_ref[...], 0.0), axis=1, keepdims=True)
        y_ref[...] = y * v

    @pl.when(pl.program_id(1) == 0)
    def _():
        w1b[...] = w1_ref[0, 0].astype(BF16)
        w3b[...] = w3_ref[0, 0].astype(BF16)
        w2b[...] = w2_ref[0, 0].astype(BF16)
        for k in range(1, n_sets):
            run(ins[2 * k], ins[2 * k + 1], outs[k])

    run(ins[0], ins[1], outs[0])


def _ffn(sets, w1, w3, w2, layer):
    d = sets[0][0].shape[1]
    f = w1.shape[-1]
    tm = min(512, sets[0][2])
    nt = sets[0][2] // tm

    def wspec(a, c):
        return pl.BlockSpec((1, 1, a, c), lambda e, m: (layer, e, 0, 0))

    in_specs = [pl.BlockSpec((tm, d), lambda e, m: (e * nt + m, 0)),
                pl.BlockSpec((tm, LANES), lambda e, m: (m, 0))]
    out_specs = [pl.BlockSpec((tm, d), lambda e, m: (e * nt + m, 0))]
    operands = [sets[0][0], sets[0][1]]
    for xg, vals, rpe in sets[1:]:
        in_specs += [pl.BlockSpec((rpe, d), lambda e, m: (e, 0)),
                     pl.BlockSpec((rpe, LANES), lambda e, m: (0, 0))]
        out_specs.append(pl.BlockSpec((rpe, d), lambda e, m: (e, 0)))
        operands += [xg, vals]
    return pl.pallas_call(
        functools.partial(_ffn_kernel, n_sets=len(sets)),
        grid=(N_EXPERTS, nt),
        in_specs=in_specs + [wspec(d, f), wspec(d, f), wspec(f, d)],
        out_specs=out_specs,
        out_shape=[jax.ShapeDtypeStruct(xg.shape, F32) for xg, _, _ in sets],
        scratch_shapes=[pltpu.VMEM((d, f), BF16), pltpu.VMEM((d, f), BF16), pltpu.VMEM((f, d), BF16)],
        compiler_params=_cparams(("parallel", "arbitrary"), 56),
        name="expert_ffn",
    )(*operands, w1, w3, w2)


def _combine_kernel(idx_ref, y_ref, xs_ref, m5_ref, g_ref, o_ref, acc_ref, *, cap, tf, eps):
    b = pl.program_id(0)
    step = pl.program_id(1)
    scatter_steps = N_EXPERTS // eps

    @pl.when(step == 0)
    def _():
        acc_ref[...] = jnp.zeros_like(acc_ref)

    @pl.when(step < scatter_steps)
    def _():
        for ei in range(eps):
            base = (b * N_EXPERTS + step * eps + ei) * cap

            def body(jg, carry, ei=ei, base=base):
                j0 = pl.multiple_of(jg * SUBLANES, SUBLANES)
                toks = [_tile_row(idx_ref[base + j0 + k]) for k in range(SUBLANES)]
                rows = [acc_ref[hi, pl.ds(lo, 1), :] for hi, lo in toks]
                for k, (hi, lo) in enumerate(toks):
                    acc_ref[hi, pl.ds(lo, 1), :] = rows[k] + y_ref[ei, jg, pl.ds(k, 1), :]
                return carry

            lax.fori_loop(0, cap // SUBLANES, body, 0)

    @pl.when(step >= scatter_steps)
    def _():
        r0 = pl.multiple_of((step - scatter_steps) * (tf // SUBLANES), tf // SUBLANES)
        moe = acc_ref[pl.ds(r0, tf // SUBLANES)].reshape(o_ref.shape)
        o_ref[...] = xs_ref[...] + m5_ref[0] * _rms(moe, g_ref[...])


def _combine(idx, y, xs, m5, post_g, batch, s, cap, shared_mod):
    d = xs.shape[1]
    tf = min(512, s)
    nfin = s // tf
    eps = _experts_per_step(cap)
    scatter_steps = N_EXPERTS // eps

    def chunk_map(b, st, idx):
        return (b * nfin + jnp.maximum(st - scatter_steps, 0), 0)

    grid_spec = pltpu.PrefetchScalarGridSpec(
        num_scalar_prefetch=1,
        grid=(batch, scatter_steps + nfin),
        in_specs=[pl.BlockSpec((eps, cap // SUBLANES, SUBLANES, d),
                               lambda b, st, idx: (jnp.minimum(st, scatter_steps - 1), b, 0, 0)),
                  pl.BlockSpec((tf, d), chunk_map),
                  pl.BlockSpec((1, 1, d), lambda b, st, idx: (0 if shared_mod else b, 0, 0)),
                  pl.BlockSpec((1, d), lambda b, st, idx: (0, 0))],
        out_specs=pl.BlockSpec((tf, d), chunk_map),
        scratch_shapes=[pltpu.VMEM((s // SUBLANES, SUBLANES, d), F32)])
    return pl.pallas_call(
        functools.partial(_combine_kernel, cap=cap, tf=tf, eps=eps),
        grid_spec=grid_spec,
        out_shape=jax.ShapeDtypeStruct(xs.shape, F32),
        compiler_params=_cparams(("arbitrary", "arbitrary"), 48),
        name="moe_combine",
    )(idx, y.reshape(N_EXPERTS, -1, SUBLANES, d), xs, m5, post_g.reshape(1, d))


def _prep_w_in(w):
    d = w.shape[0]
    kv = 2 * N_KV_HEADS * HEAD_DIM
    o_q = 2 * D_CONV + 3 * D_SHORT
    o_k = o_q + N_Q_HEADS * HEAD_DIM
    o_f = o_k + kv
    o_g = o_f + D_FOURIER
    parts = [w[:, :o_k], w[:, o_f:o_g], w[:, o_k:o_f],
             jnp.zeros((d, COL_G - COL_KV - kv), w.dtype), 0.5 * w[:, o_g:]]
    return jnp.concatenate(parts, axis=1).astype(BF16)


def _rope_tables(s):
    t = jnp.arange(s)
    row = (t // GRID_W).astype(F32)
    col = (t % GRID_W).astype(F32)
    nf = HEAD_DIM // 4
    inv = ROPE_BASE ** (-jnp.arange(nf, dtype=F32) / nf)
    ar = row[:, None] * inv
    ac = col[:, None] * inv
    cos = jnp.concatenate([jnp.cos(ar), jnp.cos(ar), jnp.cos(ac), jnp.cos(ac)], axis=1)
    sin = jnp.concatenate([-jnp.sin(ar), jnp.sin(ar), -jnp.sin(ac), jnp.sin(ac)], axis=1)
    rep = LANES // HEAD_DIM
    return jnp.tile(cos, (1, rep)), jnp.tile(sin, (1, rep))


def _dispatch(routed, batch, s):
    hp, aff, aff_t = routed
    cap = CAPACITY_FACTOR * s // N_EXPERTS
    idx, vals = _route(aff, aff_t, batch, s, cap)
    return idx, (_gather(idx, hp, batch, s, cap), vals, batch * cap)


def kernel(x, c, ctx, c_ctx, ada_w, ada_b, pre_mix_g, post_mix_g, pre_ffn_g, post_ffn_g, w_in, gate_b, conv_a_w, conv_a_b, ln_a_g, ln_a_b, w_a_out, conv_b_w, w_b_out, sink, w_c_out, w_d_out, w_o, router_w, exp_w1, exp_w3, exp_w2):
    batch, s, d = x.shape
    n_ctx = ctx.shape[1]
    depth = ada_w.shape[0]

    cvec = jnp.zeros((16, d), F32).at[:batch].set(c).at[batch].set(c_ctx)
    mod = _ada(cvec, ada_w, ada_b)
    cos_t, sin_t = _rope_tables(s)
    bd = _channel_dft_matrix()
    dft_x = _fft_tables(s)
    dft_c = _fft_tables(n_ctx)

    xs = x.reshape(batch * s, d)
    cs = ctx.reshape(batch * n_ctx, d)
    for l in range(depth):
        last = l == depth - 1
        mx = [mod[l, :batch, k * d:(k + 1) * d].reshape(batch, 1, d) for k in range(6)]
        mc = [mod[l, batch:batch + 1, k * d:(k + 1) * d].reshape(1, 1, d) for k in range(6)]
        g_pre = pre_mix_g[l].reshape(1, d)
        w = _prep_w_in(w_in[l])
        wa, wb, wc, wd = ((0.5 * t[l]).astype(BF16) for t in (w_a_out, w_b_out, w_c_out, w_d_out))
        wo = w_o[l].astype(BF16)
        rw = jnp.zeros((d, LANES), F32).at[:, :N_EXPERTS].set(router_w[l])

        p = _inproj(xs, mx[0], mx[1], g_pre, w, s)
        if last:
            kv_tile = COL_KV // INPROJ_TN * INPROJ_TN
            pc = _inproj(cs, mc[0], mc[1], g_pre, w[:, kv_tile:kv_tile + INPROJ_TN], batch * n_ctx)
            pc_kv_col = COL_KV - kv_tile
        else:
            pc = _inproj(cs, mc[0], mc[1], g_pre, w, batch * n_ctx)
            pc_kv_col = COL_KV

        def mixer(pp, att, seq, tables, xres, m, rows_per_group):
            ba = _conformer(pp, conv_a_w[l], conv_a_b[l], ln_a_g[l], ln_a_b[l], batch, seq)
            bb = _short_conv(pp, conv_b_w[l], batch, seq)
            v = _mm(pp, bd, a_cols=COL_F, out_dtype=F32)
            bf = _position_dft(v, tables, batch, seq)
            return _merge(ba, bb, att, bf, pp, gate_b[l], wa, wb, wc, wd, wo, xres,
                          m[2], m[3], m[4], post_mix_g[l], pre_ffn_g[l], rw, rows_per_group)

        att_x = _latent_attention(p, pc, pc_kv_col, cos_t, sin_t, sink[l], batch, s, n_ctx)
        xs, *routed = mixer(p, att_x, s, dft_x, xs, mx, s)
        idx_x, set_x = _dispatch(routed, batch, s)
        cap_x = CAPACITY_FACTOR * s // N_EXPERTS
        if last:
            y_x, = _ffn([set_x], exp_w1, exp_w3, exp_w2, l)
        else:
            att_c = _context_attention(pc, sink[l], batch, n_ctx)
            cs, *routed_c = mixer(pc, att_c, n_ctx, dft_c, cs, mc, batch * n_ctx)
            idx_c, set_c = _dispatch(routed_c, batch, n_ctx)
            y_x, y_c = _ffn([set_x, set_c], exp_w1, exp_w3, exp_w2, l)
            cs = _combine(idx_c, y_c, cs, mc[5], post_ffn_g[l], batch, n_ctx,
                          CAPACITY_FACTOR * n_ctx // N_EXPERTS, True)
        xs = _combine(idx_x, y_x, xs, mx[5], post_ffn_g[l], batch, s, cap_x, False)
    return xs.reshape(batch, s, d)
```

```python
import functools
import math

import jax
import jax.numpy as jnp
from jax import lax
from jax.experimental import pallas as pl
from jax.experimental.pallas import tpu as pltpu

F32 = jnp.float32
BF16 = jnp.bfloat16
I32 = jnp.int32

D_MODEL = 1024
GRID_W = 64
D_CONV = 512
CONV_K = 31
D_SHORT = 512
SHORT_K = 3
N_Q_HEADS = 8
N_KV_HEADS = 2
HEAD_DIM = 64
BLOCK = 128
D_FOURIER = 512
N_FOURIER_GROUPS = 4
N_BRANCH = 4
N_EXPERTS = 16
CAPACITY_FACTOR = 2
D_EXPERT = 1024
ROPE_BASE = 10000.0
EPS = 1e-6
NEG_INF = -1e30
LOG2E = 1.4426950408889634
MIN_NORMAL_BITS = 0x00800000

LANES = 128
SUBLANES = 8
HALO = 16
CONV_CHUNK = 256

COL_A = 0
COL_SHORT = 1024
COL_Q = 2560
COL_F = 3072
COL_KV = 3584
COL_G = 4096
N_PROJ = 8192
INPROJ_TN = 1024


def _cparams(sem, vmem_mb):
    return pltpu.CompilerParams(dimension_semantics=sem,
                                vmem_limit_bytes=vmem_mb * 1024 * 1024)


def _sigmoid(x):
    return 0.5 * jnp.tanh(0.5 * x) + 0.5


def _tanh_gate(half_x):
    return jnp.tanh(half_x) + 1.0


def _silu(x):
    return x * _sigmoid(x)


def _rms(x, g):
    return x * lax.rsqrt(jnp.mean(x * x, axis=-1, keepdims=True) + EPS) * g


def _split_bf16(x):
    hi = x.astype(BF16)
    lo = (x - hi.astype(F32)).astype(BF16)
    return hi, lo


def _dot(a, b):
    return jnp.dot(a, b, preferred_element_type=F32)


def _dot3(a, b):
    ah, al = _split_bf16(a)
    bh, bl = _split_bf16(b)
    return _dot(ah, bh) + _dot(ah, bl) + _dot(al, bh)


def _ada_kernel(c_ref, w_ref, b_ref, o_ref):
    c = c_ref[...]
    o_ref[0] = _dot3(_silu(c), w_ref[0]) + b_ref[0]


def _ada(cvec, ada_w, ada_b):
    nl, d, n6 = ada_w.shape
    rows = cvec.shape[0]
    return pl.pallas_call(
        _ada_kernel,
        grid=(nl, n6 // d),
        in_specs=[pl.BlockSpec((rows, d), lambda l, j: (0, 0)),
                  pl.BlockSpec((1, d, d), lambda l, j: (l, 0, j)),
                  pl.BlockSpec((1, 1, d), lambda l, j: (l, 0, j))],
        out_specs=pl.BlockSpec((1, rows, d), lambda l, j: (l, 0, j)),
        out_shape=jax.ShapeDtypeStruct((nl, rows, n6), F32),
        compiler_params=_cparams(("parallel", "parallel"), 40),
        name="ada_mod",
    )(cvec, ada_w, ada_b.reshape(nl, 1, n6))


def _inproj_kernel(x_ref, sh_ref, sc_ref, g_ref, w_ref, o_ref, h_ref):
    @pl.when(pl.program_id(1) == 0)
    def _():
        y = _rms(x_ref[...], g_ref[...])
        h_ref[...] = (y * (1.0 + sc_ref[0]) + sh_ref[0]).astype(BF16)

    o_ref[...] = _dot(h_ref[...], w_ref[...]).astype(BF16)


def _inproj(x2d, shift, scale, gain, w, rows_per_group):
    r, d = x2d.shape
    n = w.shape[1]
    tm = min(2048, rows_per_group)
    tn = INPROJ_TN
    tiles_per_group = rows_per_group // tm
    mod_spec = pl.BlockSpec((1, 1, d), lambda i, j: (i // tiles_per_group, 0, 0))
    return pl.pallas_call(
        _inproj_kernel,
        grid=(r // tm, n // tn),
        in_specs=[pl.BlockSpec((tm, d), lambda i, j: (i, 0)),
                  mod_spec, mod_spec,
                  pl.BlockSpec((1, d), lambda i, j: (0, 0)),
                  pl.BlockSpec((d, tn), lambda i, j: (0, j))],
        out_specs=pl.BlockSpec((tm, tn), lambda i, j: (i, j)),
        out_shape=jax.ShapeDtypeStruct((r, n), BF16),
        scratch_shapes=[pltpu.VMEM((tm, d), BF16)],
        compiler_params=_cparams(("parallel", "arbitrary"), 48),
        name="inproj",
    )(x2d, shift, scale, gain, w)


def _mm_kernel(a_ref, b_ref, o_ref):
    o_ref[...] = _dot(a_ref[...], b_ref[...]).astype(o_ref.dtype)


def _mm(a, b, *, a_cols=None, out_dtype=BF16, tm=1024):
    k, n = b.shape
    r = a.shape[0]
    cb = 0 if a_cols is None else a_cols // k
    tm = min(tm, r)
    return pl.pallas_call(
        _mm_kernel,
        grid=(r // tm,),
        in_specs=[pl.BlockSpec((tm, k), lambda i: (i, cb)),
                  pl.BlockSpec((k, n), lambda i: (0, 0))],
        out_specs=pl.BlockSpec((tm, n), lambda i: (i, 0)),
        out_shape=jax.ShapeDtypeStruct((r, n), out_dtype),
        compiler_params=_cparams(("parallel",), 40),
        name="matmul",
    )(a, b)


def _fill_window(win_ref, cur, prev, nxt, t):
    n = pl.program_id(1)
    last = pl.num_programs(1) - 1
    win_ref[HALO:HALO + t, :] = cur
    win_ref[0:HALO, :] = jnp.where(n > 0, prev, 0.0)
    win_ref[HALO + t:HALO + t + HALO, :] = jnp.where(n < last, nxt, 0.0)


def _dwconv_chunk(win_ref, t0, w_ref, ktaps):
    off = HALO - ktaps // 2
    nfull = -(-(off + ktaps) // 8) * 8
    w = win_ref[pl.ds(t0, CONV_CHUNK + nfull), :]
    acc = None
    for r in range(8):
        part = None
        for a in range(nfull // 8):
            j = 8 * a + r - off
            if 0 <= j < ktaps:
                term = w[8 * a:8 * a + CONV_CHUNK + 8] * w_ref[j:j + 1, :]
                part = term if part is None else part + term
        if part is not None:
            shifted = part[r:r + CONV_CHUNK]
            acc = shifted if acc is None else acc + shifted
    return acc


def _conformer_kernel(cur_ref, prev_ref, next_ref, cw_ref, cb_ref, lg_ref, lb_ref,
                      o_ref, win_ref, *, t):
    dc = o_ref.shape[-1]

    def glu(ref):
        blk = ref[...].astype(F32)
        return blk[:, :dc] * _sigmoid(blk[:, dc:])

    _fill_window(win_ref, glu(cur_ref), glu(prev_ref), glu(next_ref), t)

    def chunk(i, carry):
        t0 = pl.multiple_of(i * CONV_CHUNK, CONV_CHUNK)
        h = _dwconv_chunk(win_ref, t0, cw_ref, CONV_K) + cb_ref[...]
        mu = jnp.mean(h, axis=-1, keepdims=True)
        hc = h - mu
        y = hc * lax.rsqrt(jnp.mean(hc * hc, axis=-1, keepdims=True) + EPS)
        y = y * lg_ref[...] + lb_ref[...]
        o_ref[pl.ds(t0, CONV_CHUNK), :] = _silu(y).astype(BF16)
        return carry

    lax.fori_loop(0, t // CONV_CHUNK, chunk, 0)


def _halo_specs(width, col_block, s, t):
    nblk = t // HALO
    per_sample = s // HALO

    def prev_map(b, n):
        return (jnp.maximum(b * per_sample + n * nblk - 1, 0), col_block)

    def next_map(b, n):
        return (jnp.minimum(b * per_sample + (n + 1) * nblk, (b + 1) * per_sample - 1), col_block)

    return pl.BlockSpec((HALO, width), prev_map), pl.BlockSpec((HALO, width), next_map)


def _conformer(p, conv_w, conv_b, ln_g, ln_b, batch, s):
    t = min(512, s)
    nt = s // t
    prev_spec, next_spec = _halo_specs(2 * D_CONV, COL_A // (2 * D_CONV), s, t)
    vec = pl.BlockSpec((1, D_CONV), lambda b, n: (0, 0))
    return pl.pallas_call(
        functools.partial(_conformer_kernel, t=t),
        grid=(batch, nt),
        in_specs=[pl.BlockSpec((t, 2 * D_CONV), lambda b, n: (b * nt + n, COL_A // (2 * D_CONV))),
                  prev_spec, next_spec,
                  pl.BlockSpec((CONV_K, D_CONV), lambda b, n: (0, 0)),
                  vec, vec, vec],
        out_specs=pl.BlockSpec((t, D_CONV), lambda b, n: (b * nt + n, 0)),
        out_shape=jax.ShapeDtypeStruct((batch * s, D_CONV), BF16),
        scratch_shapes=[pltpu.VMEM((t + 2 * HALO, D_CONV), F32)],
        compiler_params=_cparams(("parallel", "parallel"), 40),
        name="conformer_conv",
    )(p, p, p, conv_w, conv_b.reshape(1, -1), ln_g.reshape(1, -1), ln_b.reshape(1, -1))


def _short_kernel(bg_ref, cg_ref, hv_ref, cgp_ref, hvp_ref, cgn_ref, hvn_ref, w_ref,
                  o_ref, win_ref, *, t):
    def prod(a_ref, b_ref):
        return a_ref[...].astype(F32) * b_ref[...].astype(F32)

    _fill_window(win_ref, prod(cg_ref, hv_ref), prod(cgp_ref, hvp_ref), prod(cgn_ref, hvn_ref), t)

    def chunk(i, carry):
        t0 = pl.multiple_of(i * CONV_CHUNK, CONV_CHUNK)
        h = _dwconv_chunk(win_ref, t0, w_ref, SHORT_K)
        bg = bg_ref[pl.ds(t0, CONV_CHUNK), :].astype(F32)
        o_ref[pl.ds(t0, CONV_CHUNK), :] = (bg * h).astype(BF16)
        return carry

    lax.fori_loop(0, t // CONV_CHUNK, chunk, 0)


def _short_conv(p, conv_w, batch, s):
    t = min(512, s)
    nt = s // t
    cb = COL_SHORT // D_SHORT
    cgp, cgn = _halo_specs(D_SHORT, cb + 1, s, t)
    hvp, hvn = _halo_specs(D_SHORT, cb + 2, s, t)

    def cur(k):
        return pl.BlockSpec((t, D_SHORT), lambda b, n: (b * nt + n, cb + k))

    return pl.pallas_call(
        functools.partial(_short_kernel, t=t),
        grid=(batch, nt),
        in_specs=[cur(0), cur(1), cur(2), cgp, hvp, cgn, hvn,
                  pl.BlockSpec((SHORT_K, D_SHORT), lambda b, n: (0, 0))],
        out_specs=pl.BlockSpec((t, D_SHORT), lambda b, n: (b * nt + n, 0)),
        out_shape=jax.ShapeDtypeStruct((batch * s, D_SHORT), BF16),
        scratch_shapes=[pltpu.VMEM((t + 2 * HALO, D_SHORT), F32)],
        compiler_params=_cparams(("parallel", "parallel"), 40),
        name="short_conv",
    )(p, p, p, p, p, p, p, conv_w)


def _rope(x, cos, sin):
    w = x.shape[1]
    lane = lax.broadcasted_iota(I32, x.shape, 1)
    swapped = jnp.where((lane & 31) < 16, pltpu.roll(x, w - 16, 1), pltpu.roll(x, 16, 1))
    return x * cos + swapped * sin


def _stack_heads(q, g):
    grp = N_Q_HEADS // N_KV_HEADS
    return jnp.concatenate(
        [q[:, (grp * g + i) * HEAD_DIM:(grp * g + i + 1) * HEAD_DIM] for i in range(grp)], axis=0)


def _softmax_pv(s, sink_col, v_ones, g):
    m = jnp.maximum(jnp.max(s, axis=1, keepdims=True), sink_col)
    o = _dot(jnp.exp2(s - m).astype(BF16), v_ones)
    den = o[:, LANES:LANES + 1] + jnp.exp2(sink_col - m)
    return o[:, g * HEAD_DIM:(g + 1) * HEAD_DIM] / den


def _with_ones(v):
    return jnp.concatenate([v, jnp.ones_like(v)], axis=1).astype(BF16)


def _unstack_heads(outs, nq):
    grp = N_Q_HEADS // N_KV_HEADS
    pieces = [o[i * nq:(i + 1) * nq] for o in outs for i in range(grp)]
    return jnp.concatenate(pieces, axis=1)


def _attn_kernel(q_ref, kv_ref, ckv_ref, cos_ref, sin_ref, bias_ref, sink_ref, o_ref, *, nb, qb):
    first = pl.program_id(1) * qb
    kvw = N_KV_HEADS * HEAD_DIM
    grp = N_Q_HEADS // N_KV_HEADS
    rep = N_Q_HEADS * HEAD_DIM // LANES

    q0 = pl.multiple_of(first * BLOCK, BLOCK)
    cq = cos_ref[pl.ds(q0, qb * BLOCK), :]
    sq = sin_ref[pl.ds(q0, qb * BLOCK), :]
    q = _rope(q_ref[...].astype(F32), jnp.concatenate([cq] * rep, axis=1),
              jnp.concatenate([sq] * rep, axis=1)) * (HEAD_DIM ** -0.5 * LOG2E)

    def kblock(j):
        start = pl.multiple_of(jnp.clip(first + j, 0, nb - 1) * BLOCK, BLOCK)
        kvb = kv_ref[pl.ds(start, BLOCK), :].astype(F32)
        k = _rope(kvb[:, :kvw], cos_ref[pl.ds(start, BLOCK), :], sin_ref[pl.ds(start, BLOCK), :])
        return k, kvb[:, kvw:].T

    blocks = [kblock(j) for j in range(-1, qb + 1)]
    ckv = ckv_ref[...].astype(F32)
    ctx_k = ckv[:, :kvw]
    ctx_vt = [ckv[i:i + BLOCK, kvw:].T for i in range(0, ckv.shape[0], BLOCK)]
    ones = jnp.ones((HEAD_DIM, 3 * BLOCK + ckv.shape[0]), F32)
    sinks = [sink_ref[g][:1, :] for g in range(N_KV_HEADS)]

    chains = [(i, g) for i in range(qb) for g in range(N_KV_HEADS)]
    scores = []
    for i, g in chains:
        bias_prev = bias_ref[:BLOCK, :] + jnp.where(first + i == 0, NEG_INF, 0.0)
        bias_next = bias_ref[BLOCK:, :] + jnp.where(first + i == nb - 1, NEG_INF, 0.0)
        qs = _stack_heads(q[i * BLOCK:(i + 1) * BLOCK], g).astype(BF16)
        k_all = jnp.concatenate([blocks[i][0], blocks[i + 1][0], blocks[i + 2][0], ctx_k], axis=0)
        kh = k_all[:, g * HEAD_DIM:(g + 1) * HEAD_DIM].astype(BF16)
        s = lax.dot_general(kh, qs, (((1,), (1,)), ((), ())), preferred_element_type=F32)
        scores.append(jnp.concatenate([s[:BLOCK] + bias_prev, s[BLOCK:2 * BLOCK],
                                       s[2 * BLOCK:3 * BLOCK] + bias_next, s[3 * BLOCK:]], axis=0))
    maxes = [jnp.maximum(jnp.max(s, axis=0, keepdims=True), sinks[g]) for s, (_, g) in zip(scores, chains)]
    exps = [jnp.exp2(s - m).astype(BF16) for s, m in zip(scores, maxes)]
    outs = [[] for _ in range(qb)]
    for (i, g), e, m in zip(chains, exps, maxes):
        v_t = jnp.concatenate([blocks[i][1], blocks[i + 1][1], blocks[i + 2][1]] + ctx_vt, axis=1)
        lhs = jnp.concatenate([v_t[g * HEAD_DIM:(g + 1) * HEAD_DIM], ones], axis=0).astype(BF16)
        o_t = _dot(lhs, e)
        den = o_t[HEAD_DIM:HEAD_DIM + 1] + jnp.exp2(sinks[g] - m)
        o_t = o_t * (1.0 / den)
        outs[i] += [o_t[:, h * BLOCK:(h + 1) * BLOCK].T[:, :HEAD_DIM] for h in range(grp)]
    for i in range(qb):
        o_ref[i * BLOCK:(i + 1) * BLOCK, :] = jnp.concatenate(outs[i], axis=1).astype(BF16)


def _band_bias():
    grp = N_Q_HEADS // N_KV_HEADS
    kj = jnp.arange(BLOCK)[:, None]
    qi = jnp.arange(BLOCK)[None, :]
    prev = jnp.where(qi <= kj, 0.0, NEG_INF)
    nxt = jnp.where(kj <= qi, 0.0, NEG_INF)
    return jnp.tile(jnp.concatenate([prev, nxt], axis=0).astype(F32), (1, grp))


def _sink_rows(sink):
    grp = N_Q_HEADS // N_KV_HEADS
    row = jnp.repeat(sink.astype(F32).reshape(N_KV_HEADS, grp) * LOG2E, BLOCK, axis=1)
    return jnp.broadcast_to(row[:, None, :], (N_KV_HEADS, 8, grp * BLOCK))


def _sink_cols(sink, nq):
    grp = N_Q_HEADS // N_KV_HEADS
    col = jnp.repeat(sink.astype(F32).reshape(N_KV_HEADS, grp) * LOG2E, nq, axis=1)
    return jnp.broadcast_to(col[:, :, None], (N_KV_HEADS, grp * nq, LANES))


def _latent_attention(p, pc, pc_kv_col, cos_t, sin_t, sink, batch, s, n_ctx):
    nb = s // BLOCK
    grp = N_Q_HEADS // N_KV_HEADS
    qw = N_Q_HEADS * HEAD_DIM
    kvw2 = 2 * N_KV_HEADS * HEAD_DIM
    bias = _band_bias()
    qb = 8 if nb % 8 == 0 else (4 if nb % 4 == 0 else 1)
    ns = nb // qb
    tab = pl.BlockSpec((s, LANES), lambda b, n: (0, 0))
    return pl.pallas_call(
        functools.partial(_attn_kernel, nb=nb, qb=qb),
        grid=(batch, ns),
        in_specs=[pl.BlockSpec((qb * BLOCK, qw), lambda b, n: (b * ns + n, COL_Q // qw)),
                  pl.BlockSpec((s, kvw2), lambda b, n: (b, COL_KV // kvw2)),
                  pl.BlockSpec((n_ctx, kvw2), lambda b, n: (b, pc_kv_col // kvw2)),
                  tab, tab,
                  pl.BlockSpec(bias.shape, lambda b, n: (0, 0)),
                  pl.BlockSpec((N_KV_HEADS, 8, grp * BLOCK), lambda b, n: (0, 0, 0))],
        out_specs=pl.BlockSpec((qb * BLOCK, qw), lambda b, n: (b * ns + n, 0)),
        out_shape=jax.ShapeDtypeStruct((batch * s, qw), BF16),
        compiler_params=_cparams(("parallel", "parallel"), 40),
        name="latent_attention",
    )(p, p, pc, cos_t, sin_t, bias, _sink_rows(sink))


def _cattn_kernel(q_ref, kv_ref, sink_ref, o_ref):
    nq = q_ref.shape[0]
    kvw = N_KV_HEADS * HEAD_DIM
    q = q_ref[...].astype(F32) * (HEAD_DIM ** -0.5 * LOG2E)
    kv = kv_ref[...].astype(F32)
    v_ones = _with_ones(kv[:, kvw:])
    outs = []
    for g in range(N_KV_HEADS):
        qs = _stack_heads(q, g).astype(BF16)
        kh = kv[:, g * HEAD_DIM:(g + 1) * HEAD_DIM].astype(BF16)
        s = lax.dot_general(qs, kh, (((1,), (1,)), ((), ())), preferred_element_type=F32)
        outs.append(_softmax_pv(s, sink_ref[g][:, :1], v_ones, g))
    o_ref[...] = _unstack_heads(outs, nq).astype(BF16)


def _context_attention(pc, sink, batch, n_ctx):
    grp = N_Q_HEADS // N_KV_HEADS
    qw = N_Q_HEADS * HEAD_DIM
    kvw2 = 2 * N_KV_HEADS * HEAD_DIM
    return pl.pallas_call(
        _cattn_kernel,
        grid=(batch,),
        in_specs=[pl.BlockSpec((n_ctx, qw), lambda b: (b, COL_Q // qw)),
                  pl.BlockSpec((n_ctx, kvw2), lambda b: (b, COL_KV // kvw2)),
                  pl.BlockSpec((N_KV_HEADS, grp * n_ctx, LANES), lambda b: (0, 0, 0))],
        out_specs=pl.BlockSpec((n_ctx, qw), lambda b: (b, 0)),
        out_shape=jax.ShapeDtypeStruct((batch * n_ctx, qw), BF16),
        compiler_params=_cparams(("parallel",), 40),
        name="context_attention",
    )(pc, pc, _sink_cols(sink, n_ctx))


FFT_TILE = 8


def _fft1_kernel(v_ref, w_ref, tc_ref, ts_ref, o_ref):
    df = D_FOURIER
    w = w_ref[...]
    n2 = w.shape[0] // 2
    for i in range(FFT_TILE):
        x = v_ref[:, i, :]
        z = jnp.concatenate([x[:, :df], x[:, df:]], axis=0).astype(BF16)
        c = _dot(w, z)
        cr, ci = c[:n2], c[n2:]
        tc = jnp.concatenate([tc_ref[i]] * (df // LANES), axis=1)
        ts = jnp.concatenate([ts_ref[i]] * (df // LANES), axis=1)
        o_ref[i, :, :df] = (cr * tc - ci * ts).astype(BF16)
        o_ref[i, :, df:] = (cr * ts + ci * tc).astype(BF16)


def _fft2_kernel(y_ref, w_ref, o_ref):
    df = D_FOURIER
    w = w_ref[...]
    for i in range(FFT_TILE):
        y = jnp.concatenate([y_ref[:, 2 * i * df:(2 * i + 1) * df],
                             y_ref[:, (2 * i + 1) * df:(2 * i + 2) * df]], axis=0)
        o_ref[:, i, :] = _dot(w, y)


def _fft_split(s):
    n1 = 1 << ((s.bit_length() - 1) // 2)
    return n1, s // n1


def _fft_tables(s):
    n1, n2 = _fft_split(s)
    c2, s2 = _dft_tables(n2)
    w1 = jnp.concatenate([jnp.concatenate([c2, -s2], axis=1),
                          jnp.concatenate([s2, c2], axis=1)], axis=0).astype(BF16)
    c1, s1 = _dft_tables(n1)
    w2 = jnp.concatenate([c1, -s1], axis=1).astype(BF16)
    ang = (jnp.arange(n1)[:, None] * jnp.arange(n2)[None, :]).astype(F32) * (2.0 * math.pi / s)
    tc = jnp.broadcast_to(jnp.cos(ang)[:, :, None], (n1, n2, LANES))
    ts = jnp.broadcast_to(jnp.sin(ang)[:, :, None], (n1, n2, LANES))
    return w1, w2, tc, ts


def _position_dft(v, tables, batch, s):
    w1, w2, tc, ts = tables
    n1, n2 = _fft_split(s)
    df2 = 2 * D_FOURIER
    nt1 = n1 // FFT_TILE
    nt2 = n2 // FFT_TILE
    stage1 = pl.pallas_call(
        _fft1_kernel,
        grid=(batch, nt1),
        in_specs=[pl.BlockSpec((n2, FFT_TILE, df2), lambda b, j: (b, j, 0)),
                  pl.BlockSpec(w1.shape, lambda b, j: (0, 0)),
                  pl.BlockSpec((FFT_TILE, n2, LANES), lambda b, j: (j, 0, 0)),
                  pl.BlockSpec((FFT_TILE, n2, LANES), lambda b, j: (j, 0, 0))],
        out_specs=pl.BlockSpec((FFT_TILE, n2, df2), lambda b, j: (b * nt1 + j, 0, 0)),
        out_shape=jax.ShapeDtypeStruct((batch * n1, n2, df2), BF16),
        compiler_params=_cparams(("parallel", "parallel"), 40),
        name="fft_stage1",
    )(v.reshape(batch * n2, n1, df2), w1, tc, ts)
    out = pl.pallas_call(
        _fft2_kernel,
        grid=(batch, nt2),
        in_specs=[pl.BlockSpec((n1, FFT_TILE * df2), lambda b, j: (b, j)),
                  pl.BlockSpec(w2.shape, lambda b, j: (0, 0))],
        out_specs=pl.BlockSpec((n1, FFT_TILE, D_FOURIER), lambda b, j: (b, j, 0)),
        out_shape=jax.ShapeDtypeStruct((batch * n1, n2, D_FOURIER), F32),
        compiler_params=_cparams(("parallel", "parallel"), 40),
        name="fft_stage2",
    )(stage1.reshape(batch * n1, n2 * df2), w2)
    return out.reshape(batch * s, D_FOURIER)


def _dft_tables(n):
    k = jnp.arange(n, dtype=I32)
    ang = ((k[:, None] * k[None, :]) % n).astype(F32) * (2.0 * math.pi / n)
    scale = n ** -0.5
    return jnp.cos(ang) * scale, jnp.sin(ang) * scale


def _channel_dft_matrix():
    cg = D_FOURIER // N_FOURIER_GROUPS
    c, s = _dft_tables(cg)
    eye = jnp.eye(N_FOURIER_GROUPS, dtype=F32)
    return jnp.concatenate([jnp.kron(eye, c), jnp.kron(eye, s)], axis=1).astype(BF16)


def _merge_kernel(ba_ref, bb_ref, bc_ref, bd_ref, gl_ref, gb_ref, wa_ref, wb_ref, wc_ref, wd_ref,
                  wo_ref, xs_ref, m2_ref, m3_ref, m4_ref, pg_ref, fg_ref, rw_ref,
                  xo_ref, hp_ref, aff_ref, aff_t_ref):
    tm, d = xs_ref.shape
    halves = [pl.ds(i * (tm // 2), tm // 2) for i in range(2)]
    ys = []
    for rs in halves:
        y = None
        for i, (b_ref, w_ref) in enumerate(((ba_ref, wa_ref), (bb_ref, wb_ref),
                                            (bc_ref, wc_ref), (bd_ref, wd_ref))):
            gate = _tanh_gate(gl_ref[rs, i * d:(i + 1) * d].astype(F32) + gb_ref[:, i * d:(i + 1) * d])
            term = gate * _dot(b_ref[rs, :].astype(BF16), w_ref[...])
            y = term if y is None else y + term
        ys.append(y.astype(BF16))
    zs = [_dot(y, wo_ref[...]) for y in ys]
    hs = []
    for rs, z in zip(halves, zs):
        xs = xs_ref[rs, :] + m2_ref[0] * _rms(z, pg_ref[...])
        xo_ref[rs, :] = xs
        h = _rms(xs, fg_ref[...]) * (1.0 + m4_ref[0]) + m3_ref[0]
        hp_ref[rs, :] = h.astype(BF16).astype(F32)
        hs.append(h)
    for rs, h in zip(halves, hs):
        logits = _dot3(h, rw_ref[...])
        lane = lax.broadcasted_iota(I32, logits.shape, 1)
        logits = jnp.where(lane < N_EXPERTS, logits, NEG_INF)
        e = jnp.exp(logits - jnp.max(logits, axis=1, keepdims=True))
        aff = e / jnp.sum(e, axis=1, keepdims=True)
        aff_ref[rs, :] = aff
        aff_t_ref[:, rs] = aff.T[:N_EXPERTS]


def _merge(ba, bb, bc, bd, p, gate_b, wa, wb, wc, wd, wo, xs, m2, m3, m4, post_g, ffn_g, rw,
           rows_per_group):
    r, d = xs.shape
    tm = min(512, rows_per_group)
    tiles_per_group = rows_per_group // tm
    gw = N_BRANCH * d

    def rows(width):
        return pl.BlockSpec((tm, width), lambda i: (i, 0))

    def const(shape):
        return pl.BlockSpec(shape, lambda i: (0,) * len(shape))

    mod = pl.BlockSpec((1, 1, d), lambda i: (i // tiles_per_group, 0, 0))
    half = d // 2
    return pl.pallas_call(
        _merge_kernel,
        grid=(r // tm,),
        in_specs=[rows(half), rows(half), rows(half), rows(half),
                  pl.BlockSpec((tm, gw), lambda i: (i, COL_G // gw)),
                  const((1, gw)),
                  const((half, d)), const((half, d)), const((half, d)), const((half, d)),
                  const((d, d)),
                  rows(d), mod, mod, mod, const((1, d)), const((1, d)), const((d, LANES))],
        out_specs=[rows(d), rows(d), rows(LANES),
                   pl.BlockSpec((N_EXPERTS, tm), lambda i: (0, i))],
        out_shape=[jax.ShapeDtypeStruct((r, d), F32),
                   jax.ShapeDtypeStruct((r, d), F32),
                   jax.ShapeDtypeStruct((r, LANES), F32),
                   jax.ShapeDtypeStruct((N_EXPERTS, r), F32)],
        compiler_params=_cparams(("parallel",), 48),
        name="merge_router",
    )(ba, bb, bc, bd, p, 0.5 * gate_b.reshape(1, gw), wa, wb, wc, wd, wo, xs, m2, m3, m4,
      post_g.reshape(1, d), ffn_g.reshape(1, d), rw)


def _route_kernel(aff_ref, aff_t_ref, tri_ref, tl_ref, idx_ref, val_ref, *, cap):
    s = aff_ref.shape[0]
    nslot = idx_ref.shape[-1]
    aff = aff_ref[...]

    def as_float(bits):
        return lax.bitcast_convert_type(bits, F32)

    def count(mask):
        part = jnp.sum(jnp.where(mask, 1.0, 0.0).reshape(s // 64, 64, LANES), axis=0)
        return jnp.sum(part, axis=0, keepdims=True)

    aff_t = aff_t_ref[...]

    def search(i, thr):
        cand = thr | lax.shift_left(jnp.int32(1), 30 - i)
        above = jnp.sum(jnp.where(aff_t >= as_float(cand), 1.0, 0.0), axis=1, keepdims=True)
        return jnp.where(above >= cap, cand, thr)

    thr = lax.fori_loop(0, 31, search, jnp.zeros((N_EXPERTS, 1), I32))

    def as_lane_row(col):
        block = jnp.concatenate([jnp.broadcast_to(col, (N_EXPERTS, LANES)),
                                 jnp.zeros((LANES - N_EXPERTS, LANES), F32)], axis=0)
        return block.T[0:1, :]

    gt = aff >= as_lane_row(as_float(jnp.maximum(thr + 1, MIN_NORMAL_BITS)))
    eq = (aff >= as_lane_row(as_float(thr))) & jnp.logical_not(gt)
    need = cap - count(gt)

    tri = tri_ref[...]

    def cumsum_excl(m):
        off = jnp.zeros((1, LANES), F32)
        outs = []
        for c in range(s // LANES):
            mc = m[c * LANES:(c + 1) * LANES]
            cs = _dot(tri, mc.astype(BF16))
            outs.append(cs - mc + off)
            off = off + cs[LANES - 1:LANES, :]
        return jnp.concatenate(outs, axis=0)

    eq_f = jnp.where(eq, 1.0, 0.0)
    sel = gt | (eq & (cumsum_excl(eq_f) < need))
    sel_f = jnp.where(sel, 1.0, 0.0)
    pos = jnp.where(sel, cumsum_excl(sel_f), -1.0)

    slot = lax.broadcasted_iota(I32, (s, nslot), 1).astype(F32)
    tl = tl_ref[...]
    row = lax.broadcasted_iota(I32, tl.shape, 0)
    vals = []
    for e in range(N_EXPERTS):
        onehot = jnp.where(pos[:, e:e + 1] == slot, 1.0, 0.0).astype(BF16)
        a = aff_t_ref[e:e + 1, :]
        a_hi = a.astype(BF16).astype(F32)
        a_mid = (a - a_hi).astype(BF16).astype(F32)
        a_lo = a - a_hi - a_mid
        lhs = jnp.where(row == 2, a_hi, jnp.where(row == 3, a_mid, jnp.where(row == 4, a_lo, tl)))
        res = _dot(lhs.astype(BF16), onehot)
        idx_ref[0, e:e + 1, :] = (res[0:1] * 64.0 + res[1:2] + 0.5).astype(I32)
        vals.append(res[2:3] + res[3:4] + res[4:5])
    vals = jnp.concatenate(vals + [jnp.zeros((LANES - N_EXPERTS, nslot), F32)], axis=0)
    vals_t = jnp.concatenate([vals[:, c * LANES:(c + 1) * LANES].T for c in range(nslot // LANES)],
                             axis=0)
    val_ref[...] = vals_t[:cap]


def _route(aff, aff_t, batch, s, cap):
    nslot = max(cap, LANES)
    tri = (jnp.arange(LANES)[:, None] >= jnp.arange(LANES)[None, :]).astype(BF16)
    t = jnp.arange(s)
    tl = jnp.zeros((8, s), F32).at[0].set(t // 64).at[1].set(t % 64)
    idx, vals = pl.pallas_call(
        functools.partial(_route_kernel, cap=cap),
        grid=(batch,),
        in_specs=[pl.BlockSpec((s, LANES), lambda b: (b, 0)),
                  pl.BlockSpec((N_EXPERTS, s), lambda b: (0, b)),
                  pl.BlockSpec((LANES, LANES), lambda b: (0, 0)),
                  pl.BlockSpec((8, s), lambda b: (0, 0))],
        out_specs=[pl.BlockSpec((1, N_EXPERTS, nslot), lambda b: (b, 0, 0)),
                   pl.BlockSpec((cap, LANES), lambda b: (b, 0))],
        out_shape=[jax.ShapeDtypeStruct((batch, N_EXPERTS, nslot), I32),
                   jax.ShapeDtypeStruct((batch * cap, LANES), F32)],
        compiler_params=_cparams(("parallel",), 48),
        name="route_topk",
    )(aff, aff_t, tri, tl)
    return idx[:, :, :cap].reshape(-1), vals


def _tile_row(t):
    return lax.shift_right_logical(t, SUBLANES.bit_length() - 1), t & (SUBLANES - 1)


def _experts_per_step(cap):
    return max(1, min(N_EXPERTS, 512 // cap))


def _gather_kernel(idx_ref, h_ref, xg_ref, g_ref, *, cap, eps):
    b = pl.program_id(0)
    first = pl.program_id(1) * eps
    groups = cap // SUBLANES

    for ei in range(eps):
        base = (b * N_EXPERTS + first + ei) * cap

        def body(jg, carry, ei=ei, base=base):
            j0 = pl.multiple_of(jg * SUBLANES, SUBLANES)
            for k in range(SUBLANES):
                t = idx_ref[base + j0 + k]
                hi, lo = _tile_row(t)
                g_ref[ei * groups + jg, pl.ds(k, 1), :] = h_ref[hi, pl.ds(lo, 1), :]
            return carry

        lax.fori_loop(0, groups, body, 0)
    xg_ref[...] = g_ref[...].reshape(xg_ref.shape).astype(BF16)


def _gather(idx, hp, batch, s, cap):
    d = hp.shape[1]
    eps = _experts_per_step(cap)
    grid_spec = pltpu.PrefetchScalarGridSpec(
        num_scalar_prefetch=1,
        grid=(batch, N_EXPERTS // eps),
        in_specs=[pl.BlockSpec((s // SUBLANES, SUBLANES, d), lambda b, e, idx: (b, 0, 0))],
        out_specs=pl.BlockSpec((eps, cap, d), lambda b, e, idx: (e, b, 0)),
        scratch_shapes=[pltpu.VMEM((eps * cap // SUBLANES, SUBLANES, d), F32)])
    xg = pl.pallas_call(
        functools.partial(_gather_kernel, cap=cap, eps=eps),
        grid_spec=grid_spec,
        out_shape=jax.ShapeDtypeStruct((N_EXPERTS, batch * cap, d), BF16),
        compiler_params=_cparams(("arbitrary", "arbitrary"), 48),
        name="moe_gather",
    )(idx, hp.reshape(-1, SUBLANES, d))
    return xg.reshape(N_EXPERTS * batch * cap, d)


def _ffn_kernel(*refs, n_sets):
    ins, rest = refs[:2 * n_sets], refs[2 * n_sets:]
    w1_ref, w3_ref, w2_ref = rest[:3]
    outs = rest[3:3 + n_sets]
    w1b, w3b, w2b = rest[3 + n_sets:]
    expert = pl.program_id(0)

    def run(x_ref, v_ref, y_ref):
        x = x_ref[...]
        hid = _silu(_dot(x, w1b[...])) * _dot(x, w3b[...])
        y = _dot(hid.astype(BF16), w2b[...])
        lane = lax.broadcasted_iota(I32, v_ref.shape, 1)
        v = jnp.sum(jnp.where(lane == expert, v_ref[...], 0.0), axis=1, keepdims=True)
        y_ref[...] = y * v

    @pl.when(pl.program_id(1) == 0)
    def _():
        w1b[...] = w1_ref[0, 0].astype(BF16)
        w3b[...] = w3_ref[0, 0].astype(BF16)
        w2b[...] = w2_ref[0, 0].astype(BF16)
        for k in range(1, n_sets):
            run(ins[2 * k], ins[2 * k + 1], outs[k])

    run(ins[0], ins[1], outs[0])


def _ffn(sets, w1, w3, w2, layer):
    d = sets[0][0].shape[1]
    f = w1.shape[-1]
    tm = min(1024, sets[0][2])
    nt = sets[0][2] // tm

    def wspec(a, c):
        return pl.BlockSpec((1, 1, a, c), lambda e, m: (layer, e, 0, 0))

    in_specs = [pl.BlockSpec((tm, d), lambda e, m: (e * nt + m, 0)),
                pl.BlockSpec((tm, LANES), lambda e, m: (m, 0))]
    out_specs = [pl.BlockSpec((tm, d), lambda e, m: (e * nt + m, 0))]
    operands = [sets[0][0], sets[0][1]]
    for xg, vals, rpe in sets[1:]:
        in_specs += [pl.BlockSpec((rpe, d), lambda e, m: (e, 0)),
                     pl.BlockSpec((rpe, LANES), lambda e, m: (0, 0))]
        out_specs.append(pl.BlockSpec((rpe, d), lambda e, m: (e, 0)))
        operands += [xg, vals]
    return pl.pallas_call(
        functools.partial(_ffn_kernel, n_sets=len(sets)),
        grid=(N_EXPERTS, nt),
        in_specs=in_specs + [wspec(d, f), wspec(d, f), wspec(f, d)],
        out_specs=out_specs,
        out_shape=[jax.ShapeDtypeStruct(xg.shape, F32) for xg, _, _ in sets],
        scratch_shapes=[pltpu.VMEM((d, f), BF16), pltpu.VMEM((d, f), BF16), pltpu.VMEM((f, d), BF16)],
        compiler_params=_cparams(("parallel", "arbitrary"), 56),
        name="expert_ffn",
    )(*operands, w1, w3, w2)


def _combine_kernel(idx_ref, y_ref, xs_ref, m5_ref, g_ref, o_ref, acc_ref, *, cap, tf, eps):
    b = pl.program_id(0)
    step = pl.program_id(1)
    scatter_steps = N_EXPERTS // eps

    @pl.when(step == 0)
    def _():
        acc_ref[...] = jnp.zeros_like(acc_ref)

    @pl.when(step < scatter_steps)
    def _():
        for ei in range(eps):
            base = (b * N_EXPERTS + step * eps + ei) * cap

            def body(jg, carry, ei=ei, base=base):
                j0 = pl.multiple_of(jg * SUBLANES, SUBLANES)
                toks = [_tile_row(idx_ref[base + j0 + k]) for k in range(SUBLANES)]
                rows = [acc_ref[hi, pl.ds(lo, 1), :] for hi, lo in toks]
                for k, (hi, lo) in enumerate(toks):
                    acc_ref[hi, pl.ds(lo, 1), :] = rows[k] + y_ref[ei, jg, pl.ds(k, 1), :]
                return carry

            lax.fori_loop(0, cap // SUBLANES, body, 0)

    @pl.when(step >= scatter_steps)
    def _():
        r0 = pl.multiple_of((step - scatter_steps) * (tf // SUBLANES), tf // SUBLANES)
        moe = acc_ref[pl.ds(r0, tf // SUBLANES)].reshape(o_ref.shape)
        o_ref[...] = xs_ref[...] + m5_ref[0] * _rms(moe, g_ref[...])


def _combine(idx, y, xs, m5, post_g, batch, s, cap, shared_mod):
    d = xs.shape[1]
    tf = min(512, s)
    nfin = s // tf
    eps = _experts_per_step(cap)
    scatter_steps = N_EXPERTS // eps

    def chunk_map(b, st, idx):
        return (b * nfin + jnp.maximum(st - scatter_steps, 0), 0)

    grid_spec = pltpu.PrefetchScalarGridSpec(
        num_scalar_prefetch=1,
        grid=(batch, scatter_steps + nfin),
        in_specs=[pl.BlockSpec((eps, cap // SUBLANES, SUBLANES, d),
                               lambda b, st, idx: (jnp.minimum(st, scatter_steps - 1), b, 0, 0)),
                  pl.BlockSpec((tf, d), chunk_map),
                  pl.BlockSpec((1, 1, d), lambda b, st, idx: (0 if shared_mod else b, 0, 0)),
                  pl.BlockSpec((1, d), lambda b, st, idx: (0, 0))],
        out_specs=pl.BlockSpec((tf, d), chunk_map),
        scratch_shapes=[pltpu.VMEM((s // SUBLANES, SUBLANES, d), F32)])
    return pl.pallas_call(
        functools.partial(_combine_kernel, cap=cap, tf=tf, eps=eps),
        grid_spec=grid_spec,
        out_shape=jax.ShapeDtypeStruct(xs.shape, F32),
        compiler_params=_cparams(("arbitrary", "arbitrary"), 48),
        name="moe_combine",
    )(idx, y.reshape(N_EXPERTS, -1, SUBLANES, d), xs, m5, post_g.reshape(1, d))


def _prep_w_in(w):
    d = w.shape[0]
    kv = 2 * N_KV_HEADS * HEAD_DIM
    o_q = 2 * D_CONV + 3 * D_SHORT
    o_k = o_q + N_Q_HEADS * HEAD_DIM
    o_f = o_k + kv
    o_g = o_f + D_FOURIER
    parts = [w[:, :o_k], w[:, o_f:o_g], w[:, o_k:o_f],
             jnp.zeros((d, COL_G - COL_KV - kv), w.dtype), 0.5 * w[:, o_g:]]
    return jnp.concatenate(parts, axis=1).astype(BF16)


def _rope_tables(s):
    t = jnp.arange(s)
    row = (t // GRID_W).astype(F32)
    col = (t % GRID_W).astype(F32)
    nf = HEAD_DIM // 4
    inv = ROPE_BASE ** (-jnp.arange(nf, dtype=F32) / nf)
    ar = row[:, None] * inv
    ac = col[:, None] * inv
    cos = jnp.concatenate([jnp.cos(ar), jnp.cos(ar), jnp.cos(ac), jnp.cos(ac)], axis=1)
    sin = jnp.concatenate([-jnp.sin(ar), jnp.sin(ar), -jnp.sin(ac), jnp.sin(ac)], axis=1)
    rep = LANES // HEAD_DIM
    return jnp.tile(cos, (1, rep)), jnp.tile(sin, (1, rep))


def _dispatch(routed, batch, s):
    hp, aff, aff_t = routed
    cap = CAPACITY_FACTOR * s // N_EXPERTS
    idx, vals = _route(aff, aff_t, batch, s, cap)
    return idx, (_gather(idx, hp, batch, s, cap), vals, batch * cap)


def kernel(x, c, ctx, c_ctx, ada_w, ada_b, pre_mix_g, post_mix_g, pre_ffn_g, post_ffn_g, w_in, gate_b, conv_a_w, conv_a_b, ln_a_g, ln_a_b, w_a_out, conv_b_w, w_b_out, sink, w_c_out, w_d_out, w_o, router_w, exp_w1, exp_w3, exp_w2):
    batch, s, d = x.shape
    n_ctx = ctx.shape[1]
    depth = ada_w.shape[0]

    cvec = jnp.zeros((16, d), F32).at[:batch].set(c).at[batch].set(c_ctx)
    mod = _ada(cvec, ada_w, ada_b)
    cos_t, sin_t = _rope_tables(s)
    bd = _channel_dft_matrix()
    dft_x = _fft_tables(s)
    dft_c = _fft_tables(n_ctx)

    xs = x.reshape(batch * s, d)
    cs = ctx.reshape(batch * n_ctx, d)
    for l in range(depth):
        last = l == depth - 1
        mx = [mod[l, :batch, k * d:(k + 1) * d].reshape(batch, 1, d) for k in range(6)]
        mc = [mod[l, batch:batch + 1, k * d:(k + 1) * d].reshape(1, 1, d) for k in range(6)]
        g_pre = pre_mix_g[l].reshape(1, d)
        w = _prep_w_in(w_in[l])
        wa, wb, wc, wd = ((0.5 * t[l]).astype(BF16) for t in (w_a_out, w_b_out, w_c_out, w_d_out))
        wo = w_o[l].astype(BF16)
        rw = jnp.zeros((d, LANES), F32).at[:, :N_EXPERTS].set(router_w[l])

        p = _inproj(xs, mx[0], mx[1], g_pre, w, s)
        if last:
            kv_tile = COL_KV // INPROJ_TN * INPROJ_TN
            pc = _inproj(cs, mc[0], mc[1], g_pre, w[:, kv_tile:kv_tile + INPROJ_TN], batch * n_ctx)
            pc_kv_col = COL_KV - kv_tile
        else:
            pc = _inproj(cs, mc[0], mc[1], g_pre, w, batch * n_ctx)
            pc_kv_col = COL_KV

        def mixer(pp, att, seq, tables, xres, m, rows_per_group):
            ba = _conformer(pp, conv_a_w[l], conv_a_b[l], ln_a_g[l], ln_a_b[l], batch, seq)
            bb = _short_conv(pp, conv_b_w[l], batch, seq)
            v = _mm(pp, bd, a_cols=COL_F, out_dtype=F32)
            bf = _position_dft(v, tables, batch, seq)
            return _merge(ba, bb, att, bf, pp, gate_b[l], wa, wb, wc, wd, wo, xres,
                          m[2], m[3], m[4], post_mix_g[l], pre_ffn_g[l], rw, rows_per_group)

        att_x = _latent_attention(p, pc, pc_kv_col, cos_t, sin_t, sink[l], batch, s, n_ctx)
        xs, *routed = mixer(p, att_x, s, dft_x, xs, mx, s)
        idx_x, set_x = _dispatch(routed, batch, s)
        cap_x = CAPACITY_FACTOR * s // N_EXPERTS
        if last:
            y_x, = _ffn([set_x], exp_w1, exp_w3, exp_w2, l)
        else:
            att_c = _context_attention(pc, sink[l], batch, n_ctx)
            cs, *routed_c = mixer(pc, att_c, n_ctx, dft_c, cs, mc, batch * n_ctx)
            idx_c, set_c = _dispatch(routed_c, batch, n_ctx)
            y_x, y_c = _ffn([set_x, set_c], exp_w1, exp_w3, exp_w2, l)
            cs = _combine(idx_c, y_c, cs, mc[5], post_ffn_g[l], batch, n_ctx,
                          CAPACITY_FACTOR * n_ctx // N_EXPERTS, True)
        xs = _combine(idx_x, y_x, xs, mx[5], post_ffn_g[l], batch, s, cap_x, False)
    return xs.reshape(batch, s, d)
```

```python
import functools
import math

import jax
import jax.numpy as jnp
from jax import lax
from jax.experimental import pallas as pl
from jax.experimental.pallas import tpu as pltpu

F32 = jnp.float32
BF16 = jnp.bfloat16
I32 = jnp.int32

D_MODEL = 1024
GRID_W = 64
D_CONV = 512
CONV_K = 31
D_SHORT = 512
SHORT_K = 3
N_Q_HEADS = 8
N_KV_HEADS = 2
HEAD_DIM = 64
BLOCK = 128
D_FOURIER = 512
N_FOURIER_GROUPS = 4
N_BRANCH = 4
N_EXPERTS = 16
CAPACITY_FACTOR = 2
D_EXPERT = 1024
ROPE_BASE = 10000.0
EPS = 1e-6
NEG_INF = -1e30
LOG2E = 1.4426950408889634
MIN_NORMAL_BITS = 0x00800000

LANES = 128
SUBLANES = 8
HALO = 16
CONV_CHUNK = 256

COL_A = 0
COL_SHORT = 1024
COL_Q = 2560
COL_F = 3072
COL_KV = 3584
COL_G = 4096
N_PROJ = 8192
INPROJ_TN = 2048
MERGE_PARTS = 2


def _cparams(sem, vmem_mb):
    return pltpu.CompilerParams(dimension_semantics=sem,
                                vmem_limit_bytes=vmem_mb * 1024 * 1024)


def _sigmoid(x):
    return 0.5 * jnp.tanh(0.5 * x) + 0.5


def _tanh_gate(half_x):
    return jnp.tanh(half_x) + 1.0


def _silu(x):
    return x * _sigmoid(x)


def _rms(x, g):
    return x * lax.rsqrt(jnp.mean(x * x, axis=-1, keepdims=True) + EPS) * g


def _split_bf16(x):
    hi = x.astype(BF16)
    lo = (x - hi.astype(F32)).astype(BF16)
    return hi, lo


def _dot(a, b):
    return jnp.dot(a, b, preferred_element_type=F32)


def _dot3(a, b):
    ah, al = _split_bf16(a)
    bh, bl = _split_bf16(b)
    return _dot(ah, bh) + _dot(ah, bl) + _dot(al, bh)


def _ada_kernel(c_ref, w_ref, b_ref, o_ref):
    c = c_ref[...]
    o_ref[0] = _dot3(_silu(c), w_ref[0]) + b_ref[0]


def _ada(cvec, ada_w, ada_b):
    nl, d, n6 = ada_w.shape
    rows = cvec.shape[0]
    return pl.pallas_call(
        _ada_kernel,
        grid=(nl, n6 // d),
        in_specs=[pl.BlockSpec((rows, d), lambda l, j: (0, 0)),
                  pl.BlockSpec((1, d, d), lambda l, j: (l, 0, j)),
                  pl.BlockSpec((1, 1, d), lambda l, j: (l, 0, j))],
        out_specs=pl.BlockSpec((1, rows, d), lambda l, j: (l, 0, j)),
        out_shape=jax.ShapeDtypeStruct((nl, rows, n6), F32),
        compiler_params=_cparams(("parallel", "parallel"), 40),
        name="ada_mod",
    )(cvec, ada_w, ada_b.reshape(nl, 1, n6))


def _inproj_kernel(x_ref, sh_ref, sc_ref, g_ref, w_ref, o_ref, h_ref):
    @pl.when(pl.program_id(1) == 0)
    def _():
        y = _rms(x_ref[...], g_ref[...])
        h_ref[...] = (y * (1.0 + sc_ref[0]) + sh_ref[0]).astype(BF16)

    o_ref[...] = _dot(h_ref[...], w_ref[...]).astype(BF16)


def _inproj(x2d, shift, scale, gain, w, rows_per_group):
    r, d = x2d.shape
    n = w.shape[1]
    tm = min(2048, rows_per_group)
    tn = INPROJ_TN
    tiles_per_group = rows_per_group // tm
    mod_spec = pl.BlockSpec((1, 1, d), lambda i, j: (i // tiles_per_group, 0, 0))
    return pl.pallas_call(
        _inproj_kernel,
        grid=(r // tm, n // tn),
        in_specs=[pl.BlockSpec((tm, d), lambda i, j: (i, 0)),
                  mod_spec, mod_spec,
                  pl.BlockSpec((1, d), lambda i, j: (0, 0)),
                  pl.BlockSpec((d, tn), lambda i, j: (0, j))],
        out_specs=pl.BlockSpec((tm, tn), lambda i, j: (i, j)),
        out_shape=jax.ShapeDtypeStruct((r, n), BF16),
        scratch_shapes=[pltpu.VMEM((tm, d), BF16)],
        compiler_params=_cparams(("parallel", "arbitrary"), 56),
        name="inproj",
    )(x2d, shift, scale, gain, w)


def _mm_kernel(a_ref, b_ref, o_ref):
    o_ref[...] = _dot(a_ref[...], b_ref[...]).astype(o_ref.dtype)


def _mm(a, b, *, a_cols=None, out_dtype=BF16, tm=1024):
    k, n = b.shape
    r = a.shape[0]
    cb = 0 if a_cols is None else a_cols // k
    tm = min(tm, r)
    return pl.pallas_call(
        _mm_kernel,
        grid=(r // tm,),
        in_specs=[pl.BlockSpec((tm, k), lambda i: (i, cb)),
                  pl.BlockSpec((k, n), lambda i: (0, 0))],
        out_specs=pl.BlockSpec((tm, n), lambda i: (i, 0)),
        out_shape=jax.ShapeDtypeStruct((r, n), out_dtype),
        compiler_params=_cparams(("parallel",), 40),
        name="matmul",
    )(a, b)


def _fill_window(win_ref, cur, prev, nxt, t):
    n = pl.program_id(1)
    last = pl.num_programs(1) - 1
    win_ref[HALO:HALO + t, :] = cur
    win_ref[0:HALO, :] = jnp.where(n > 0, prev, 0.0)
    win_ref[HALO + t:HALO + t + HALO, :] = jnp.where(n < last, nxt, 0.0)


def _dwconv_chunk(win_ref, t0, w_ref, ktaps):
    off = HALO - ktaps // 2
    nfull = -(-(off + ktaps) // 8) * 8
    w = win_ref[pl.ds(t0, CONV_CHUNK + nfull), :]
    acc = None
    for r in range(8):
        part = None
        for a in range(nfull // 8):
            j = 8 * a + r - off
            if 0 <= j < ktaps:
                term = w[8 * a:8 * a + CONV_CHUNK + 8] * w_ref[j:j + 1, :]
                part = term if part is None else part + term
        if part is not None:
            shifted = part[r:r + CONV_CHUNK]
            acc = shifted if acc is None else acc + shifted
    return acc


def _conformer_kernel(cur_ref, prev_ref, next_ref, cw_ref, cb_ref, lg_ref, lb_ref,
                      o_ref, win_ref, *, t):
    dc = o_ref.shape[-1]

    def glu(ref):
        blk = ref[...].astype(F32)
        return blk[:, :dc] * _sigmoid(blk[:, dc:])

    _fill_window(win_ref, glu(cur_ref), glu(prev_ref), glu(next_ref), t)

    def chunk(i, carry):
        t0 = pl.multiple_of(i * CONV_CHUNK, CONV_CHUNK)
        h = _dwconv_chunk(win_ref, t0, cw_ref, CONV_K) + cb_ref[...]
        mu = jnp.mean(h, axis=-1, keepdims=True)
        hc = h - mu
        y = hc * lax.rsqrt(jnp.mean(hc * hc, axis=-1, keepdims=True) + EPS)
        y = y * lg_ref[...] + lb_ref[...]
        o_ref[pl.ds(t0, CONV_CHUNK), :] = _silu(y).astype(BF16)
        return carry

    lax.fori_loop(0, t // CONV_CHUNK, chunk, 0)


def _halo_specs(width, col_block, s, t):
    nblk = t // HALO
    per_sample = s // HALO

    def prev_map(b, n):
        return (jnp.maximum(b * per_sample + n * nblk - 1, 0), col_block)

    def next_map(b, n):
        return (jnp.minimum(b * per_sample + (n + 1) * nblk, (b + 1) * per_sample - 1), col_block)

    return pl.BlockSpec((HALO, width), prev_map), pl.BlockSpec((HALO, width), next_map)


def _conformer(p, conv_w, conv_b, ln_g, ln_b, batch, s):
    t = min(1024, s)
    nt = s // t
    prev_spec, next_spec = _halo_specs(2 * D_CONV, COL_A // (2 * D_CONV), s, t)
    vec = pl.BlockSpec((1, D_CONV), lambda b, n: (0, 0))
    return pl.pallas_call(
        functools.partial(_conformer_kernel, t=t),
        grid=(batch, nt),
        in_specs=[pl.BlockSpec((t, 2 * D_CONV), lambda b, n: (b * nt + n, COL_A // (2 * D_CONV))),
                  prev_spec, next_spec,
                  pl.BlockSpec((CONV_K, D_CONV), lambda b, n: (0, 0)),
                  vec, vec, vec],
        out_specs=pl.BlockSpec((t, D_CONV), lambda b, n: (b * nt + n, 0)),
        out_shape=jax.ShapeDtypeStruct((batch * s, D_CONV), BF16),
        scratch_shapes=[pltpu.VMEM((t + 2 * HALO, D_CONV), F32)],
        compiler_params=_cparams(("parallel", "parallel"), 40),
        name="conformer_conv",
    )(p, p, p, conv_w, conv_b.reshape(1, -1), ln_g.reshape(1, -1), ln_b.reshape(1, -1))


def _short_kernel(bg_ref, cg_ref, hv_ref, cgp_ref, hvp_ref, cgn_ref, hvn_ref, w_ref,
                  o_ref, win_ref, *, t):
    def prod(a_ref, b_ref):
        return a_ref[...].astype(F32) * b_ref[...].astype(F32)

    _fill_window(win_ref, prod(cg_ref, hv_ref), prod(cgp_ref, hvp_ref), prod(cgn_ref, hvn_ref), t)

    def chunk(i, carry):
        t0 = pl.multiple_of(i * CONV_CHUNK, CONV_CHUNK)
        h = _dwconv_chunk(win_ref, t0, w_ref, SHORT_K)
        bg = bg_ref[pl.ds(t0, CONV_CHUNK), :].astype(F32)
        o_ref[pl.ds(t0, CONV_CHUNK), :] = (bg * h).astype(BF16)
        return carry

    lax.fori_loop(0, t // CONV_CHUNK, chunk, 0)


def _short_conv(p, conv_w, batch, s):
    t = min(1024, s)
    nt = s // t
    cb = COL_SHORT // D_SHORT
    cgp, cgn = _halo_specs(D_SHORT, cb + 1, s, t)
    hvp, hvn = _halo_specs(D_SHORT, cb + 2, s, t)

    def cur(k):
        return pl.BlockSpec((t, D_SHORT), lambda b, n: (b * nt + n, cb + k))

    return pl.pallas_call(
        functools.partial(_short_kernel, t=t),
        grid=(batch, nt),
        in_specs=[cur(0), cur(1), cur(2), cgp, hvp, cgn, hvn,
                  pl.BlockSpec((SHORT_K, D_SHORT), lambda b, n: (0, 0))],
        out_specs=pl.BlockSpec((t, D_SHORT), lambda b, n: (b * nt + n, 0)),
        out_shape=jax.ShapeDtypeStruct((batch * s, D_SHORT), BF16),
        scratch_shapes=[pltpu.VMEM((t + 2 * HALO, D_SHORT), F32)],
        compiler_params=_cparams(("parallel", "parallel"), 40),
        name="short_conv",
    )(p, p, p, p, p, p, p, conv_w)


def _rope(x, cos, sin):
    w = x.shape[1]
    lane = lax.broadcasted_iota(I32, x.shape, 1)
    swapped = jnp.where((lane & 31) < 16, pltpu.roll(x, w - 16, 1), pltpu.roll(x, 16, 1))
    return x * cos + swapped * sin


def _stack_heads(q, g):
    grp = N_Q_HEADS // N_KV_HEADS
    return jnp.concatenate(
        [q[:, (grp * g + i) * HEAD_DIM:(grp * g + i + 1) * HEAD_DIM] for i in range(grp)], axis=0)


def _softmax_pv(s, sink_col, v_ones, g):
    m = jnp.maximum(jnp.max(s, axis=1, keepdims=True), sink_col)
    o = _dot(jnp.exp2(s - m).astype(BF16), v_ones)
    den = o[:, LANES:LANES + 1] + jnp.exp2(sink_col - m)
    return o[:, g * HEAD_DIM:(g + 1) * HEAD_DIM] / den


def _with_ones(v):
    return jnp.concatenate([v, jnp.ones_like(v)], axis=1).astype(BF16)


def _unstack_heads(outs, nq):
    grp = N_Q_HEADS // N_KV_HEADS
    pieces = [o[i * nq:(i + 1) * nq] for o in outs for i in range(grp)]
    return jnp.concatenate(pieces, axis=1)


def _attn_kernel(q_ref, kv_ref, ckv_ref, cos_ref, sin_ref, bias_ref, sink_ref, o_ref, *, nb, qb):
    first = pl.program_id(1) * qb
    kvw = N_KV_HEADS * HEAD_DIM
    grp = N_Q_HEADS // N_KV_HEADS
    rep = N_Q_HEADS * HEAD_DIM // LANES

    q0 = pl.multiple_of(first * BLOCK, BLOCK)
    cq = cos_ref[pl.ds(q0, qb * BLOCK), :]
    sq = sin_ref[pl.ds(q0, qb * BLOCK), :]
    q = _rope(q_ref[...].astype(F32), jnp.concatenate([cq] * rep, axis=1),
              jnp.concatenate([sq] * rep, axis=1)) * (HEAD_DIM ** -0.5 * LOG2E)

    def kblock(j):
        start = pl.multiple_of(jnp.clip(first + j, 0, nb - 1) * BLOCK, BLOCK)
        kvb = kv_ref[pl.ds(start, BLOCK), :].astype(F32)
        k = _rope(kvb[:, :kvw], cos_ref[pl.ds(start, BLOCK), :], sin_ref[pl.ds(start, BLOCK), :])
        return k, kvb[:, kvw:].T

    blocks = [kblock(j) for j in range(-1, qb + 1)]
    ckv = ckv_ref[...].astype(F32)
    ctx_k = ckv[:, :kvw]
    ctx_vt = [ckv[i:i + BLOCK, kvw:].T for i in range(0, ckv.shape[0], BLOCK)]
    ones = jnp.ones((HEAD_DIM, 3 * BLOCK + ckv.shape[0]), F32)
    sinks = [sink_ref[g][:1, :] for g in range(N_KV_HEADS)]

    chains = [(i, g) for i in range(qb) for g in range(N_KV_HEADS)]
    scores = []
    for i, g in chains:
        bias_prev = bias_ref[:BLOCK, :] + jnp.where(first + i == 0, NEG_INF, 0.0)
        bias_next = bias_ref[BLOCK:, :] + jnp.where(first + i == nb - 1, NEG_INF, 0.0)
        qs = _stack_heads(q[i * BLOCK:(i + 1) * BLOCK], g).astype(BF16)
        k_all = jnp.concatenate([blocks[i][0], blocks[i + 1][0], blocks[i + 2][0], ctx_k], axis=0)
        kh = k_all[:, g * HEAD_DIM:(g + 1) * HEAD_DIM].astype(BF16)
        s = lax.dot_general(kh, qs, (((1,), (1,)), ((), ())), preferred_element_type=F32)
        scores.append(jnp.concatenate([s[:BLOCK] + bias_prev, s[BLOCK:2 * BLOCK],
                                       s[2 * BLOCK:3 * BLOCK] + bias_next, s[3 * BLOCK:]], axis=0))
    maxes = [jnp.maximum(jnp.max(s, axis=0, keepdims=True), sinks[g]) for s, (_, g) in zip(scores, chains)]
    exps = [jnp.exp2(s - m).astype(BF16) for s, m in zip(scores, maxes)]
    outs = [[] for _ in range(qb)]
    for (i, g), e, m in zip(chains, exps, maxes):
        v_t = jnp.concatenate([blocks[i][1], blocks[i + 1][1], blocks[i + 2][1]] + ctx_vt, axis=1)
        lhs = jnp.concatenate([v_t[g * HEAD_DIM:(g + 1) * HEAD_DIM], ones], axis=0).astype(BF16)
        o_t = _dot(lhs, e)
        den = o_t[HEAD_DIM:HEAD_DIM + 1] + jnp.exp2(sinks[g] - m)
        o_t = o_t * (1.0 / den)
        outs[i] += [o_t[:, h * BLOCK:(h + 1) * BLOCK].T[:, :HEAD_DIM] for h in range(grp)]
    for i in range(qb):
        o_ref[i * BLOCK:(i + 1) * BLOCK, :] = jnp.concatenate(outs[i], axis=1).astype(BF16)


def _band_bias():
    grp = N_Q_HEADS // N_KV_HEADS
    kj = jnp.arange(BLOCK)[:, None]
    qi = jnp.arange(BLOCK)[None, :]
    prev = jnp.where(qi <= kj, 0.0, NEG_INF)
    nxt = jnp.where(kj <= qi, 0.0, NEG_INF)
    return jnp.tile(jnp.concatenate([prev, nxt], axis=0).astype(F32), (1, grp))


def _sink_rows(sink):
    grp = N_Q_HEADS // N_KV_HEADS
    row = jnp.repeat(sink.astype(F32).reshape(N_KV_HEADS, grp) * LOG2E, BLOCK, axis=1)
    return jnp.broadcast_to(row[:, None, :], (N_KV_HEADS, 8, grp * BLOCK))


def _sink_cols(sink, nq):
    grp = N_Q_HEADS // N_KV_HEADS
    col = jnp.repeat(sink.astype(F32).reshape(N_KV_HEADS, grp) * LOG2E, nq, axis=1)
    return jnp.broadcast_to(col[:, :, None], (N_KV_HEADS, grp * nq, LANES))


def _latent_attention(p, pc, pc_kv_col, cos_t, sin_t, sink, batch, s, n_ctx):
    nb = s // BLOCK
    grp = N_Q_HEADS // N_KV_HEADS
    qw = N_Q_HEADS * HEAD_DIM
    kvw2 = 2 * N_KV_HEADS * HEAD_DIM
    bias = _band_bias()
    qb = 8 if nb % 8 == 0 else (4 if nb % 4 == 0 else 1)
    ns = nb // qb
    tab = pl.BlockSpec((s, LANES), lambda b, n: (0, 0))
    return pl.pallas_call(
        functools.partial(_attn_kernel, nb=nb, qb=qb),
        grid=(batch, ns),
        in_specs=[pl.BlockSpec((qb * BLOCK, qw), lambda b, n: (b * ns + n, COL_Q // qw)),
                  pl.BlockSpec((s, kvw2), lambda b, n: (b, COL_KV // kvw2)),
                  pl.BlockSpec((n_ctx, kvw2), lambda b, n: (b, pc_kv_col // kvw2)),
                  tab, tab,
                  pl.BlockSpec(bias.shape, lambda b, n: (0, 0)),
                  pl.BlockSpec((N_KV_HEADS, 8, grp * BLOCK), lambda b, n: (0, 0, 0))],
        out_specs=pl.BlockSpec((qb * BLOCK, qw), lambda b, n: (b * ns + n, 0)),
        out_shape=jax.ShapeDtypeStruct((batch * s, qw), BF16),
        compiler_params=_cparams(("parallel", "parallel"), 40),
        name="latent_attention",
    )(p, p, pc, cos_t, sin_t, bias, _sink_rows(sink))


def _cattn_kernel(q_ref, kv_ref, sink_ref, o_ref):
    nq = q_ref.shape[0]
    kvw = N_KV_HEADS * HEAD_DIM
    q = q_ref[...].astype(F32) * (HEAD_DIM ** -0.5 * LOG2E)
    kv = kv_ref[...].astype(F32)
    v_ones = _with_ones(kv[:, kvw:])
    outs = []
    for g in range(N_KV_HEADS):
        qs = _stack_heads(q, g).astype(BF16)
        kh = kv[:, g * HEAD_DIM:(g + 1) * HEAD_DIM].astype(BF16)
        s = lax.dot_general(qs, kh, (((1,), (1,)), ((), ())), preferred_element_type=F32)
        outs.append(_softmax_pv(s, sink_ref[g][:, :1], v_ones, g))
    o_ref[...] = _unstack_heads(outs, nq).astype(BF16)


def _context_attention(pc, sink, batch, n_ctx):
    grp = N_Q_HEADS // N_KV_HEADS
    qw = N_Q_HEADS * HEAD_DIM
    kvw2 = 2 * N_KV_HEADS * HEAD_DIM
    return pl.pallas_call(
        _cattn_kernel,
        grid=(batch,),
        in_specs=[pl.BlockSpec((n_ctx, qw), lambda b: (b, COL_Q // qw)),
                  pl.BlockSpec((n_ctx, kvw2), lambda b: (b, COL_KV // kvw2)),
                  pl.BlockSpec((N_KV_HEADS, grp * n_ctx, LANES), lambda b: (0, 0, 0))],
        out_specs=pl.BlockSpec((n_ctx, qw), lambda b: (b, 0)),
        out_shape=jax.ShapeDtypeStruct((batch * n_ctx, qw), BF16),
        compiler_params=_cparams(("parallel",), 40),
        name="context_attention",
    )(pc, pc, _sink_cols(sink, n_ctx))


FFT_TILE = 16


def _fft1_kernel(v_ref, w_ref, tc_ref, ts_ref, o_ref):
    df = D_FOURIER
    w = w_ref[...]
    n2 = w.shape[0] // 2
    for i in range(FFT_TILE):
        x = v_ref[:, i, :]
        z = jnp.concatenate([x[:, :df], x[:, df:]], axis=0).astype(BF16)
        c = _dot(w, z)
        cr, ci = c[:n2], c[n2:]
        tc = jnp.concatenate([tc_ref[i]] * (df // LANES), axis=1)
        ts = jnp.concatenate([ts_ref[i]] * (df // LANES), axis=1)
        o_ref[i, :, :df] = (cr * tc - ci * ts).astype(BF16)
        o_ref[i, :, df:] = (cr * ts + ci * tc).astype(BF16)


def _fft2_kernel(y_ref, w_ref, o_ref):
    df = D_FOURIER
    w = w_ref[...]
    for i in range(FFT_TILE):
        y = jnp.concatenate([y_ref[:, 2 * i * df:(2 * i + 1) * df],
                             y_ref[:, (2 * i + 1) * df:(2 * i + 2) * df]], axis=0)
        o_ref[:, i, :] = _dot(w, y)


def _fft_split(s):
    n1 = 1 << ((s.bit_length() - 1) // 2)
    return n1, s // n1


def _fft_tables(s):
    n1, n2 = _fft_split(s)
    c2, s2 = _dft_tables(n2)
    w1 = jnp.concatenate([jnp.concatenate([c2, -s2], axis=1),
                          jnp.concatenate([s2, c2], axis=1)], axis=0).astype(BF16)
    c1, s1 = _dft_tables(n1)
    w2 = jnp.concatenate([c1, -s1], axis=1).astype(BF16)
    ang = (jnp.arange(n1)[:, None] * jnp.arange(n2)[None, :]).astype(F32) * (2.0 * math.pi / s)
    tc = jnp.broadcast_to(jnp.cos(ang)[:, :, None], (n1, n2, LANES))
    ts = jnp.broadcast_to(jnp.sin(ang)[:, :, None], (n1, n2, LANES))
    return w1, w2, tc, ts


def _position_dft(v, tables, batch, s):
    w1, w2, tc, ts = tables
    n1, n2 = _fft_split(s)
    df2 = 2 * D_FOURIER
    nt1 = n1 // FFT_TILE
    nt2 = n2 // FFT_TILE
    stage1 = pl.pallas_call(
        _fft1_kernel,
        grid=(batch, nt1),
        in_specs=[pl.BlockSpec((n2, FFT_TILE, df2), lambda b, j: (b, j, 0)),
                  pl.BlockSpec(w1.shape, lambda b, j: (0, 0)),
                  pl.BlockSpec((FFT_TILE, n2, LANES), lambda b, j: (j, 0, 0)),
                  pl.BlockSpec((FFT_TILE, n2, LANES), lambda b, j: (j, 0, 0))],
        out_specs=pl.BlockSpec((FFT_TILE, n2, df2), lambda b, j: (b * nt1 + j, 0, 0)),
        out_shape=jax.ShapeDtypeStruct((batch * n1, n2, df2), BF16),
        compiler_params=_cparams(("parallel", "parallel"), 40),
        name="fft_stage1",
    )(v.reshape(batch * n2, n1, df2), w1, tc, ts)
    out = pl.pallas_call(
        _fft2_kernel,
        grid=(batch, nt2),
        in_specs=[pl.BlockSpec((n1, FFT_TILE * df2), lambda b, j: (b, j)),
                  pl.BlockSpec(w2.shape, lambda b, j: (0, 0))],
        out_specs=pl.BlockSpec((n1, FFT_TILE, D_FOURIER), lambda b, j: (b, j, 0)),
        out_shape=jax.ShapeDtypeStruct((batch * n1, n2, D_FOURIER), F32),
        compiler_params=_cparams(("parallel", "parallel"), 40),
        name="fft_stage2",
    )(stage1.reshape(batch * n1, n2 * df2), w2)
    return out.reshape(batch * s, D_FOURIER)


def _dft_tables(n):
    k = jnp.arange(n, dtype=I32)
    ang = ((k[:, None] * k[None, :]) % n).astype(F32) * (2.0 * math.pi / n)
    scale = n ** -0.5
    return jnp.cos(ang) * scale, jnp.sin(ang) * scale


def _channel_dft_matrix():
    cg = D_FOURIER // N_FOURIER_GROUPS
    c, s = _dft_tables(cg)
    eye = jnp.eye(N_FOURIER_GROUPS, dtype=F32)
    return jnp.concatenate([jnp.kron(eye, c), jnp.kron(eye, s)], axis=1).astype(BF16)


def _merge_kernel(ba_ref, bb_ref, bc_ref, bd_ref, gl_ref, gb_ref, wa_ref, wb_ref, wc_ref, wd_ref,
                  wo_ref, xs_ref, m2_ref, m3_ref, m4_ref, pg_ref, fg_ref, rw_ref,
                  xo_ref, hp_ref, aff_ref, aff_t_ref):
    tm, d = xs_ref.shape
    halves = [pl.ds(i * (tm // MERGE_PARTS), tm // MERGE_PARTS) for i in range(MERGE_PARTS)]
    ys = []
    for rs in halves:
        y = None
        for i, (b_ref, w_ref) in enumerate(((ba_ref, wa_ref), (bb_ref, wb_ref),
                                            (bc_ref, wc_ref), (bd_ref, wd_ref))):
            gate = _tanh_gate(gl_ref[rs, i * d:(i + 1) * d].astype(F32) + gb_ref[:, i * d:(i + 1) * d])
            term = gate * _dot(b_ref[rs, :].astype(BF16), w_ref[...])
            y = term if y is None else y + term
        ys.append(y.astype(BF16))
    zs = [_dot(y, wo_ref[...]) for y in ys]
    hs = []
    for rs, z in zip(halves, zs):
        xs = xs_ref[rs, :] + m2_ref[0] * _rms(z, pg_ref[...])
        xo_ref[rs, :] = xs
        h = _rms(xs, fg_ref[...]) * (1.0 + m4_ref[0]) + m3_ref[0]
        hp_ref[rs, :] = h.astype(BF16).astype(F32)
        hs.append(h)
    for rs, h in zip(halves, hs):
        logits = _dot3(h, rw_ref[...])
        lane = lax.broadcasted_iota(I32, logits.shape, 1)
        logits = jnp.where(lane < N_EXPERTS, logits, NEG_INF)
        e = jnp.exp(logits - jnp.max(logits, axis=1, keepdims=True))
        aff = e / jnp.sum(e, axis=1, keepdims=True)
        aff_ref[rs, :] = aff
        aff_t_ref[:, rs] = aff.T[:N_EXPERTS]


def _merge(ba, bb, bc, bd, p, gate_b, wa, wb, wc, wd, wo, xs, m2, m3, m4, post_g, ffn_g, rw,
           rows_per_group):
    r, d = xs.shape
    tm = min(512, rows_per_group)
    tiles_per_group = rows_per_group // tm
    gw = N_BRANCH * d

    def rows(width):
        return pl.BlockSpec((tm, width), lambda i: (i, 0))

    def const(shape):
        return pl.BlockSpec(shape, lambda i: (0,) * len(shape))

    mod = pl.BlockSpec((1, 1, d), lambda i: (i // tiles_per_group, 0, 0))
    half = d // 2
    return pl.pallas_call(
        _merge_kernel,
        grid=(r // tm,),
        in_specs=[rows(half), rows(half), rows(half), rows(half),
                  pl.BlockSpec((tm, gw), lambda i: (i, COL_G // gw)),
                  const((1, gw)),
                  const((half, d)), const((half, d)), const((half, d)), const((half, d)),
                  const((d, d)),
                  rows(d), mod, mod, mod, const((1, d)), const((1, d)), const((d, LANES))],
        out_specs=[rows(d), rows(d), rows(LANES),
                   pl.BlockSpec((N_EXPERTS, tm), lambda i: (0, i))],
        out_shape=[jax.ShapeDtypeStruct((r, d), F32),
                   jax.ShapeDtypeStruct((r, d), F32),
                   jax.ShapeDtypeStruct((r, LANES), F32),
                   jax.ShapeDtypeStruct((N_EXPERTS, r), F32)],
        compiler_params=_cparams(("parallel",), 48),
        name="merge_router",
    )(ba, bb, bc, bd, p, 0.5 * gate_b.reshape(1, gw), wa, wb, wc, wd, wo, xs, m2, m3, m4,
      post_g.reshape(1, d), ffn_g.reshape(1, d), rw)


def _route_kernel(aff_ref, aff_t_ref, tri_ref, tl_ref, idx_ref, val_ref, *, cap):
    s = aff_ref.shape[0]
    nslot = idx_ref.shape[-1]
    aff = aff_ref[...]

    def as_float(bits):
        return lax.bitcast_convert_type(bits, F32)

    def count(mask):
        part = jnp.sum(jnp.where(mask, 1.0, 0.0).reshape(s // 64, 64, LANES), axis=0)
        return jnp.sum(part, axis=0, keepdims=True)

    aff_t = aff_t_ref[...]

    def search(i, thr):
        cand = thr | lax.shift_left(jnp.int32(1), 30 - i)
        above = jnp.sum(jnp.where(aff_t >= as_float(cand), 1.0, 0.0), axis=1, keepdims=True)
        return jnp.where(above >= cap, cand, thr)

    thr = lax.fori_loop(0, 31, search, jnp.zeros((N_EXPERTS, 1), I32))

    def as_lane_row(col):
        block = jnp.concatenate([jnp.broadcast_to(col, (N_EXPERTS, LANES)),
                                 jnp.zeros((LANES - N_EXPERTS, LANES), F32)], axis=0)
        return block.T[0:1, :]

    gt = aff >= as_lane_row(as_float(jnp.maximum(thr + 1, MIN_NORMAL_BITS)))
    eq = (aff >= as_lane_row(as_float(thr))) & jnp.logical_not(gt)
    need = cap - count(gt)

    tri = tri_ref[...]

    def cumsum_excl(m):
        off = jnp.zeros((1, LANES), F32)
        outs = []
        for c in range(s // LANES):
            mc = m[c * LANES:(c + 1) * LANES]
            cs = _dot(tri, mc.astype(BF16))
            outs.append(cs - mc + off)
            off = off + cs[LANES - 1:LANES, :]
        return jnp.concatenate(outs, axis=0)

    eq_f = jnp.where(eq, 1.0, 0.0)
    sel = gt | (eq & (cumsum_excl(eq_f) < need))
    sel_f = jnp.where(sel, 1.0, 0.0)
    pos = jnp.where(sel, cumsum_excl(sel_f), -1.0)

    slot = lax.broadcasted_iota(I32, (s, nslot), 1).astype(F32)
    tl = tl_ref[...]
    row = lax.broadcasted_iota(I32, tl.shape, 0)
    vals = []
    for e in range(N_EXPERTS):
        onehot = jnp.where(pos[:, e:e + 1] == slot, 1.0, 0.0).astype(BF16)
        a = aff_t_ref[e:e + 1, :]
        a_hi = a.astype(BF16).astype(F32)
        a_mid = (a - a_hi).astype(BF16).astype(F32)
        a_lo = a - a_hi - a_mid
        lhs = jnp.where(row == 2, a_hi, jnp.where(row == 3, a_mid, jnp.where(row == 4, a_lo, tl)))
        res = _dot(lhs.astype(BF16), onehot)
        idx_ref[0, e:e + 1, :] = (res[0:1] * 64.0 + res[1:2] + 0.5).astype(I32)
        vals.append(res[2:3] + res[3:4] + res[4:5])
    vals = jnp.concatenate(vals + [jnp.zeros((LANES - N_EXPERTS, nslot), F32)], axis=0)
    vals_t = jnp.concatenate([vals[:, c * LANES:(c + 1) * LANES].T for c in range(nslot // LANES)],
                             axis=0)
    val_ref[...] = vals_t[:cap]


def _route(aff, aff_t, batch, s, cap):
    nslot = max(cap, LANES)
    tri = (jnp.arange(LANES)[:, None] >= jnp.arange(LANES)[None, :]).astype(BF16)
    t = jnp.arange(s)
    tl = jnp.zeros((8, s), F32).at[0].set(t // 64).at[1].set(t % 64)
    idx, vals = pl.pallas_call(
        functools.partial(_route_kernel, cap=cap),
        grid=(batch,),
        in_specs=[pl.BlockSpec((s, LANES), lambda b: (b, 0)),
                  pl.BlockSpec((N_EXPERTS, s), lambda b: (0, b)),
                  pl.BlockSpec((LANES, LANES), lambda b: (0, 0)),
                  pl.BlockSpec((8, s), lambda b: (0, 0))],
        out_specs=[pl.BlockSpec((1, N_EXPERTS, nslot), lambda b: (b, 0, 0)),
                   pl.BlockSpec((cap, LANES), lambda b: (b, 0))],
        out_shape=[jax.ShapeDtypeStruct((batch, N_EXPERTS, nslot), I32),
                   jax.ShapeDtypeStruct((batch * cap, LANES), F32)],
        compiler_params=_cparams(("parallel",), 48),
        name="route_topk",
    )(aff, aff_t, tri, tl)
    return idx[:, :, :cap].reshape(-1), vals


def _tile_row(t):
    return lax.shift_right_logical(t, SUBLANES.bit_length() - 1), t & (SUBLANES - 1)


def _experts_per_step(cap):
    return max(1, min(N_EXPERTS, 512 // cap))


def _gather_kernel(idx_ref, h_ref, xg_ref, g_ref, *, cap, eps):
    b = pl.program_id(0)
    first = pl.program_id(1) * eps
    groups = cap // SUBLANES

    for ei in range(eps):
        base = (b * N_EXPERTS + first + ei) * cap

        def body(jg, carry, ei=ei, base=base):
            j0 = pl.multiple_of(jg * SUBLANES, SUBLANES)
            for k in range(SUBLANES):
                t = idx_ref[base + j0 + k]
                hi, lo = _tile_row(t)
                g_ref[ei * groups + jg, pl.ds(k, 1), :] = h_ref[hi, pl.ds(lo, 1), :]
            return carry

        lax.fori_loop(0, groups, body, 0)
    xg_ref[...] = g_ref[...].reshape(xg_ref.shape).astype(BF16)


def _gather(idx, hp, batch, s, cap):
    d = hp.shape[1]
    eps = _experts_per_step(cap)
    grid_spec = pltpu.PrefetchScalarGridSpec(
        num_scalar_prefetch=1,
        grid=(batch, N_EXPERTS // eps),
        in_specs=[pl.BlockSpec((s // SUBLANES, SUBLANES, d), lambda b, e, idx: (b, 0, 0))],
        out_specs=pl.BlockSpec((eps, cap, d), lambda b, e, idx: (e, b, 0)),
        scratch_shapes=[pltpu.VMEM((eps * cap // SUBLANES, SUBLANES, d), F32)])
    xg = pl.pallas_call(
        functools.partial(_gather_kernel, cap=cap, eps=eps),
        grid_spec=grid_spec,
        out_shape=jax.ShapeDtypeStruct((N_EXPERTS, batch * cap, d), BF16),
        compiler_params=_cparams(("arbitrary", "arbitrary"), 48),
        name="moe_gather",
    )(idx, hp.reshape(-1, SUBLANES, d))
    return xg.reshape(N_EXPERTS * batch * cap, d)


def _ffn_kernel(*refs, n_sets):
    ins, rest = refs[:2 * n_sets], refs[2 * n_sets:]
    w1_ref, w3_ref, w2_ref = rest[:3]
    outs = rest[3:3 + n_sets]
    w1b, w3b, w2b = rest[3 + n_sets:]
    expert = pl.program_id(0)

    def run(x_ref, v_ref, y_ref):
        x = x_ref[...]
        hid = _silu(_dot(x, w1b[...])) * _dot(x, w3b[...])
        y = _dot(hid.astype(BF16), w2b[...])
        lane = lax.broadcasted_iota(I32, v_ref.shape, 1)
        v = jnp.sum(jnp.where(lane == expert, v_ref[...], 0.0), axis=1, keepdims=True)
        y_ref[...] = y * v

    @pl.when(pl.program_id(1) == 0)
    def _():
        w1b[...] = w1_ref[0, 0].astype(BF16)
        w3b[...] = w3_ref[0, 0].astype(BF16)
        w2b[...] = w2_ref[0, 0].astype(BF16)
        for k in range(1, n_sets):
            run(ins[2 * k], ins[2 * k + 1], outs[k])

    run(ins[0], ins[1], outs[0])


def _ffn(sets, w1, w3, w2, layer):
    d = sets[0][0].shape[1]
    f = w1.shape[-1]
    tm = min(1024, sets[0][2])
    nt = sets[0][2] // tm

    def wspec(a, c):
        return pl.BlockSpec((1, 1, a, c), lambda e, m: (layer, e, 0, 0))

    in_specs = [pl.BlockSpec((tm, d), lambda e, m: (e * nt + m, 0)),
                pl.BlockSpec((tm, LANES), lambda e, m: (m, 0))]
    out_specs = [pl.BlockSpec((tm, d), lambda e, m: (e * nt + m, 0))]
    operands = [sets[0][0], sets[0][1]]
    for xg, vals, rpe in sets[1:]:
        in_specs += [pl.BlockSpec((rpe, d), lambda e, m: (e, 0)),
                     pl.BlockSpec((rpe, LANES), lambda e, m: (0, 0))]
        out_specs.append(pl.BlockSpec((rpe, d), lambda e, m: (e, 0)))
        operands += [xg, vals]
    return pl.pallas_call(
        functools.partial(_ffn_kernel, n_sets=len(sets)),
        grid=(N_EXPERTS, nt),
        in_specs=in_specs + [wspec(d, f), wspec(d, f), wspec(f, d)],
        out_specs=out_specs,
        out_shape=[jax.ShapeDtypeStruct(xg.shape, F32) for xg, _, _ in sets],
        scratch_shapes=[pltpu.VMEM((d, f), BF16), pltpu.VMEM((d, f), BF16), pltpu.VMEM((f, d), BF16)],
        compiler_params=_cparams(("parallel", "arbitrary"), 56),
        name="expert_ffn",
    )(*operands, w1, w3, w2)


def _combine_kernel(idx_ref, y_ref, xs_ref, m5_ref, g_ref, o_ref, acc_ref, *, cap, tf, eps):
    b = pl.program_id(0)
    step = pl.program_id(1)
    scatter_steps = N_EXPERTS // eps

    @pl.when(step == 0)
    def _():
        acc_ref[...] = jnp.zeros_like(acc_ref)

    @pl.when(step < scatter_steps)
    def _():
        for ei in range(eps):
            base = (b * N_EXPERTS + step * eps + ei) * cap

            def body(jg, carry, ei=ei, base=base):
                j0 = pl.multiple_of(jg * SUBLANES, SUBLANES)
                toks = [_tile_row(idx_ref[base + j0 + k]) for k in range(SUBLANES)]
                rows = [acc_ref[hi, pl.ds(lo, 1), :] for hi, lo in toks]
                for k, (hi, lo) in enumerate(toks):
                    acc_ref[hi, pl.ds(lo, 1), :] = rows[k] + y_ref[ei, jg, pl.ds(k, 1), :]
                return carry

            lax.fori_loop(0, cap // SUBLANES, body, 0)

    @pl.when(step >= scatter_steps)
    def _():
        r0 = pl.multiple_of((step - scatter_steps) * (tf // SUBLANES), tf // SUBLANES)
        moe = acc_ref[pl.ds(r0, tf // SUBLANES)].reshape(o_ref.shape)
        o_ref[...] = xs_ref[...] + m5_ref[0] * _rms(moe, g_ref[...])


def _combine(idx, y, xs, m5, post_g, batch, s, cap, shared_mod):
    d = xs.shape[1]
    tf = min(512, s)
    nfin = s // tf
    eps = _experts_per_step(cap)
    scatter_steps = N_EXPERTS // eps

    def chunk_map(b, st, idx):
        return (b * nfin + jnp.maximum(st - scatter_steps, 0), 0)

    grid_spec = pltpu.PrefetchScalarGridSpec(
        num_scalar_prefetch=1,
        grid=(batch, scatter_steps + nfin),
        in_specs=[pl.BlockSpec((eps, cap // SUBLANES, SUBLANES, d),
                               lambda b, st, idx: (jnp.minimum(st, scatter_steps - 1), b, 0, 0)),
                  pl.BlockSpec((tf, d), chunk_map),
                  pl.BlockSpec((1, 1, d), lambda b, st, idx: (0 if shared_mod else b, 0, 0)),
                  pl.BlockSpec((1, d), lambda b, st, idx: (0, 0))],
        out_specs=pl.BlockSpec((tf, d), chunk_map),
        scratch_shapes=[pltpu.VMEM((s // SUBLANES, SUBLANES, d), F32)])
    return pl.pallas_call(
        functools.partial(_combine_kernel, cap=cap, tf=tf, eps=eps),
        grid_spec=grid_spec,
        out_shape=jax.ShapeDtypeStruct(xs.shape, F32),
        compiler_params=_cparams(("arbitrary", "arbitrary"), 48),
        name="moe_combine",
    )(idx, y.reshape(N_EXPERTS, -1, SUBLANES, d), xs, m5, post_g.reshape(1, d))


def _prep_w_in(w):
    d = w.shape[0]
    kv = 2 * N_KV_HEADS * HEAD_DIM
    o_q = 2 * D_CONV + 3 * D_SHORT
    o_k = o_q + N_Q_HEADS * HEAD_DIM
    o_f = o_k + kv
    o_g = o_f + D_FOURIER
    parts = [w[:, :o_k], w[:, o_f:o_g], w[:, o_k:o_f],
             jnp.zeros((d, COL_G - COL_KV - kv), w.dtype), 0.5 * w[:, o_g:]]
    return jnp.concatenate(parts, axis=1).astype(BF16)


def _rope_tables(s):
    t = jnp.arange(s)
    row = (t // GRID_W).astype(F32)
    col = (t % GRID_W).astype(F32)
    nf = HEAD_DIM // 4
    inv = ROPE_BASE ** (-jnp.arange(nf, dtype=F32) / nf)
    ar = row[:, None] * inv
    ac = col[:, None] * inv
    cos = jnp.concatenate([jnp.cos(ar), jnp.cos(ar), jnp.cos(ac), jnp.cos(ac)], axis=1)
    sin = jnp.concatenate([-jnp.sin(ar), jnp.sin(ar), -jnp.sin(ac), jnp.sin(ac)], axis=1)
    rep = LANES // HEAD_DIM
    return jnp.tile(cos, (1, rep)), jnp.tile(sin, (1, rep))


def _dispatch(routed, batch, s):
    hp, aff, aff_t = routed
    cap = CAPACITY_FACTOR * s // N_EXPERTS
    idx, vals = _route(aff, aff_t, batch, s, cap)
    return idx, (_gather(idx, hp, batch, s, cap), vals, batch * cap)


def kernel(x, c, ctx, c_ctx, ada_w, ada_b, pre_mix_g, post_mix_g, pre_ffn_g, post_ffn_g, w_in, gate_b, conv_a_w, conv_a_b, ln_a_g, ln_a_b, w_a_out, conv_b_w, w_b_out, sink, w_c_out, w_d_out, w_o, router_w, exp_w1, exp_w3, exp_w2):
    batch, s, d = x.shape
    n_ctx = ctx.shape[1]
    depth = ada_w.shape[0]

    cvec = jnp.zeros((16, d), F32).at[:batch].set(c).at[batch].set(c_ctx)
    mod = _ada(cvec, ada_w, ada_b)
    cos_t, sin_t = _rope_tables(s)
    bd = _channel_dft_matrix()
    dft_x = _fft_tables(s)
    dft_c = _fft_tables(n_ctx)

    xs = x.reshape(batch * s, d)
    cs = ctx.reshape(batch * n_ctx, d)
    for l in range(depth):
        last = l == depth - 1
        mx = [mod[l, :batch, k * d:(k + 1) * d].reshape(batch, 1, d) for k in range(6)]
        mc = [mod[l, batch:batch + 1, k * d:(k + 1) * d].reshape(1, 1, d) for k in range(6)]
        g_pre = pre_mix_g[l].reshape(1, d)
        w = _prep_w_in(w_in[l])
        wa, wb, wc, wd = ((0.5 * t[l]).astype(BF16) for t in (w_a_out, w_b_out, w_c_out, w_d_out))
        wo = w_o[l].astype(BF16)
        rw = jnp.zeros((d, LANES), F32).at[:, :N_EXPERTS].set(router_w[l])

        p = _inproj(xs, mx[0], mx[1], g_pre, w, s)
        if last:
            kv_tile = COL_KV // INPROJ_TN * INPROJ_TN
            pc = _inproj(cs, mc[0], mc[1], g_pre, w[:, kv_tile:kv_tile + INPROJ_TN], batch * n_ctx)
            pc_kv_col = COL_KV - kv_tile
        else:
            pc = _inproj(cs, mc[0], mc[1], g_pre, w, batch * n_ctx)
            pc_kv_col = COL_KV

        def mixer(pp, att, seq, tables, xres, m, rows_per_group):
            ba = _conformer(pp, conv_a_w[l], conv_a_b[l], ln_a_g[l], ln_a_b[l], batch, seq)
            bb = _short_conv(pp, conv_b_w[l], batch, seq)
            v = _mm(pp, bd, a_cols=COL_F, out_dtype=F32)
            bf = _position_dft(v, tables, batch, seq)
            return _merge(ba, bb, att, bf, pp, gate_b[l], wa, wb, wc, wd, wo, xres,
                          m[2], m[3], m[4], post_mix_g[l], pre_ffn_g[l], rw, rows_per_group)

        att_x = _latent_attention(p, pc, pc_kv_col, cos_t, sin_t, sink[l], batch, s, n_ctx)
        xs, *routed = mixer(p, att_x, s, dft_x, xs, mx, s)
        idx_x, set_x = _dispatch(routed, batch, s)
        cap_x = CAPACITY_FACTOR * s // N_EXPERTS
        if last:
            y_x, = _ffn([set_x], exp_w1, exp_w3, exp_w2, l)
        else:
            att_c = _context_attention(pc, sink[l], batch, n_ctx)
            cs, *routed_c = mixer(pc, att_c, n_ctx, dft_c, cs, mc, batch * n_ctx)
            idx_c, set_c = _dispatch(routed_c, batch, n_ctx)
            y_x, y_c = _ffn([set_x, set_c], exp_w1, exp_w3, exp_w2, l)
            cs = _combine(idx_c, y_c, cs, mc[5], post_ffn_g[l], batch, n_ctx,
                          CAPACITY_FACTOR * n_ctx // N_EXPERTS, True)
        xs = _combine(idx_x, y_x, xs, mx[5], post_ffn_g[l], batch, s, cap_x, False)
    return xs.reshape(batch, s, d)
```

```python
import functools
import math

import jax
import jax.numpy as jnp
from jax import lax
from jax.experimental import pallas as pl
from jax.experimental.pallas import tpu as pltpu

F32 = jnp.float32
BF16 = jnp.bfloat16
I32 = jnp.int32

D_MODEL = 1024
GRID_W = 64
D_CONV = 512
CONV_K = 31
D_SHORT = 512
SHORT_K = 3
N_Q_HEADS = 8
N_KV_HEADS = 2
HEAD_DIM = 64
BLOCK = 128
D_FOURIER = 512
N_FOURIER_GROUPS = 4
N_BRANCH = 4
N_EXPERTS = 16
CAPACITY_FACTOR = 2
D_EXPERT = 1024
ROPE_BASE = 10000.0
EPS = 1e-6
NEG_INF = -1e30
LOG2E = 1.4426950408889634
MIN_NORMAL_BITS = 0x00800000

LANES = 128
SUBLANES = 8
HALO = 16
CONV_CHUNK = 256

COL_A = 0
COL_SHORT = 1024
COL_Q = 2560
COL_F = 3072
COL_KV = 3584
COL_G = 4096
N_PROJ = 8192
INPROJ_TN = 2048
MERGE_PARTS = 2


def _cparams(sem, vmem_mb):
    return pltpu.CompilerParams(dimension_semantics=sem,
                                vmem_limit_bytes=vmem_mb * 1024 * 1024)


def _sigmoid(x):
    return 0.5 * jnp.tanh(0.5 * x) + 0.5


def _tanh_gate(half_x):
    return jnp.tanh(half_x) + 1.0


def _silu(x):
    return x * _sigmoid(x)


def _rms(x, g):
    return x * lax.rsqrt(jnp.mean(x * x, axis=-1, keepdims=True) + EPS) * g


def _split_bf16(x):
    hi = x.astype(BF16)
    lo = (x - hi.astype(F32)).astype(BF16)
    return hi, lo


def _dot(a, b):
    return jnp.dot(a, b, preferred_element_type=F32)


def _dot3(a, b):
    ah, al = _split_bf16(a)
    bh, bl = _split_bf16(b)
    return _dot(ah, bh) + _dot(ah, bl) + _dot(al, bh)


def _ada_kernel(c_ref, w_ref, b_ref, o_ref):
    c = c_ref[...]
    o_ref[0] = _dot3(_silu(c), w_ref[0]) + b_ref[0]


def _ada(cvec, ada_w, ada_b):
    nl, d, n6 = ada_w.shape
    rows = cvec.shape[0]
    return pl.pallas_call(
        _ada_kernel,
        grid=(nl, n6 // d),
        in_specs=[pl.BlockSpec((rows, d), lambda l, j: (0, 0)),
                  pl.BlockSpec((1, d, d), lambda l, j: (l, 0, j)),
                  pl.BlockSpec((1, 1, d), lambda l, j: (l, 0, j))],
        out_specs=pl.BlockSpec((1, rows, d), lambda l, j: (l, 0, j)),
        out_shape=jax.ShapeDtypeStruct((nl, rows, n6), F32),
        compiler_params=_cparams(("parallel", "parallel"), 40),
        name="ada_mod",
    )(cvec, ada_w, ada_b.reshape(nl, 1, n6))


def _inproj_kernel(x_ref, sh_ref, sc_ref, g_ref, w_ref, o_ref, h_ref):
    @pl.when(pl.program_id(1) == 0)
    def _():
        y = _rms(x_ref[...], g_ref[...])
        h_ref[...] = (y * (1.0 + sc_ref[0]) + sh_ref[0]).astype(BF16)

    o_ref[...] = _dot(h_ref[...], w_ref[...]).astype(BF16)


def _inproj(x2d, shift, scale, gain, w, rows_per_group):
    r, d = x2d.shape
    n = w.shape[1]
    tm = min(2048, rows_per_group)
    tn = INPROJ_TN
    tiles_per_group = rows_per_group // tm
    mod_spec = pl.BlockSpec((1, 1, d), lambda i, j: (i // tiles_per_group, 0, 0))
    return pl.pallas_call(
        _inproj_kernel,
        grid=(r // tm, n // tn),
        in_specs=[pl.BlockSpec((tm, d), lambda i, j: (i, 0)),
                  mod_spec, mod_spec,
                  pl.BlockSpec((1, d), lambda i, j: (0, 0)),
                  pl.BlockSpec((d, tn), lambda i, j: (0, j))],
        out_specs=pl.BlockSpec((tm, tn), lambda i, j: (i, j)),
        out_shape=jax.ShapeDtypeStruct((r, n), BF16),
        scratch_shapes=[pltpu.VMEM((tm, d), BF16)],
        compiler_params=_cparams(("parallel", "arbitrary"), 56),
        name="inproj",
    )(x2d, shift, scale, gain, w)


def _mm_kernel(a_ref, b_ref, o_ref):
    o_ref[...] = _dot(a_ref[...], b_ref[...]).astype(o_ref.dtype)


def _mm(a, b, *, a_cols=None, out_dtype=BF16, tm=1024):
    k, n = b.shape
    r = a.shape[0]
    cb = 0 if a_cols is None else a_cols // k
    tm = min(tm, r)
    return pl.pallas_call(
        _mm_kernel,
        grid=(r // tm,),
        in_specs=[pl.BlockSpec((tm, k), lambda i: (i, cb)),
                  pl.BlockSpec((k, n), lambda i: (0, 0))],
        out_specs=pl.BlockSpec((tm, n), lambda i: (i, 0)),
        out_shape=jax.ShapeDtypeStruct((r, n), out_dtype),
        compiler_params=_cparams(("parallel",), 40),
        name="matmul",
    )(a, b)


def _fill_window(win_ref, cur, prev, nxt, t):
    n = pl.program_id(1)
    last = pl.num_programs(1) - 1
    win_ref[HALO:HALO + t, :] = cur
    win_ref[0:HALO, :] = jnp.where(n > 0, prev, 0.0)
    win_ref[HALO + t:HALO + t + HALO, :] = jnp.where(n < last, nxt, 0.0)


def _dwconv_chunk(win_ref, t0, w_ref, ktaps):
    off = HALO - ktaps // 2
    nfull = -(-(off + ktaps) // 8) * 8
    w = win_ref[pl.ds(t0, CONV_CHUNK + nfull), :]
    acc = None
    for r in range(8):
        part = None
        for a in range(nfull // 8):
            j = 8 * a + r - off
            if 0 <= j < ktaps:
                term = w[8 * a:8 * a + CONV_CHUNK + 8] * w_ref[j:j + 1, :]
                part = term if part is None else part + term
        if part is not None:
            shifted = part[r:r + CONV_CHUNK]
            acc = shifted if acc is None else acc + shifted
    return acc


def _conformer_kernel(cur_ref, prev_ref, next_ref, cw_ref, cb_ref, lg_ref, lb_ref,
                      o_ref, win_ref, *, t):
    dc = o_ref.shape[-1]

    def glu(ref):
        blk = ref[...].astype(F32)
        return blk[:, :dc] * _sigmoid(blk[:, dc:])

    _fill_window(win_ref, glu(cur_ref), glu(prev_ref), glu(next_ref), t)

    def chunk(i, carry):
        t0 = pl.multiple_of(i * CONV_CHUNK, CONV_CHUNK)
        h = _dwconv_chunk(win_ref, t0, cw_ref, CONV_K) + cb_ref[...]
        mu = jnp.mean(h, axis=-1, keepdims=True)
        hc = h - mu
        y = hc * lax.rsqrt(jnp.mean(hc * hc, axis=-1, keepdims=True) + EPS)
        y = y * lg_ref[...] + lb_ref[...]
        o_ref[pl.ds(t0, CONV_CHUNK), :] = _silu(y).astype(BF16)
        return carry

    lax.fori_loop(0, t // CONV_CHUNK, chunk, 0)


def _halo_specs(width, col_block, s, t):
    nblk = t // HALO
    per_sample = s // HALO

    def prev_map(b, n):
        return (jnp.maximum(b * per_sample + n * nblk - 1, 0), col_block)

    def next_map(b, n):
        return (jnp.minimum(b * per_sample + (n + 1) * nblk, (b + 1) * per_sample - 1), col_block)

    return pl.BlockSpec((HALO, width), prev_map), pl.BlockSpec((HALO, width), next_map)


def _conformer(p, conv_w, conv_b, ln_g, ln_b, batch, s):
    t = min(1024, s)
    nt = s // t
    prev_spec, next_spec = _halo_specs(2 * D_CONV, COL_A // (2 * D_CONV), s, t)
    vec = pl.BlockSpec((1, D_CONV), lambda b, n: (0, 0))
    return pl.pallas_call(
        functools.partial(_conformer_kernel, t=t),
        grid=(batch, nt),
        in_specs=[pl.BlockSpec((t, 2 * D_CONV), lambda b, n: (b * nt + n, COL_A // (2 * D_CONV))),
                  prev_spec, next_spec,
                  pl.BlockSpec((CONV_K, D_CONV), lambda b, n: (0, 0)),
                  vec, vec, vec],
        out_specs=pl.BlockSpec((t, D_CONV), lambda b, n: (b * nt + n, 0)),
        out_shape=jax.ShapeDtypeStruct((batch * s, D_CONV), BF16),
        scratch_shapes=[pltpu.VMEM((t + 2 * HALO, D_CONV), F32)],
        compiler_params=_cparams(("parallel", "parallel"), 40),
        name="conformer_conv",
    )(p, p, p, conv_w, conv_b.reshape(1, -1), ln_g.reshape(1, -1), ln_b.reshape(1, -1))


def _short_kernel(bg_ref, cg_ref, hv_ref, cgp_ref, hvp_ref, cgn_ref, hvn_ref, w_ref,
                  o_ref, win_ref, *, t):
    def prod(a_ref, b_ref):
        return a_ref[...].astype(F32) * b_ref[...].astype(F32)

    _fill_window(win_ref, prod(cg_ref, hv_ref), prod(cgp_ref, hvp_ref), prod(cgn_ref, hvn_ref), t)

    def chunk(i, carry):
        t0 = pl.multiple_of(i * CONV_CHUNK, CONV_CHUNK)
        h = _dwconv_chunk(win_ref, t0, w_ref, SHORT_K)
        bg = bg_ref[pl.ds(t0, CONV_CHUNK), :].astype(F32)
        o_ref[pl.ds(t0, CONV_CHUNK), :] = (bg * h).astype(BF16)
        return carry

    lax.fori_loop(0, t // CONV_CHUNK, chunk, 0)


def _short_conv(p, conv_w, batch, s):
    t = min(1024, s)
    nt = s // t
    cb = COL_SHORT // D_SHORT
    cgp, cgn = _halo_specs(D_SHORT, cb + 1, s, t)
    hvp, hvn = _halo_specs(D_SHORT, cb + 2, s, t)

    def cur(k):
        return pl.BlockSpec((t, D_SHORT), lambda b, n: (b * nt + n, cb + k))

    return pl.pallas_call(
        functools.partial(_short_kernel, t=t),
        grid=(batch, nt),
        in_specs=[cur(0), cur(1), cur(2), cgp, hvp, cgn, hvn,
                  pl.BlockSpec((SHORT_K, D_SHORT), lambda b, n: (0, 0))],
        out_specs=pl.BlockSpec((t, D_SHORT), lambda b, n: (b * nt + n, 0)),
        out_shape=jax.ShapeDtypeStruct((batch * s, D_SHORT), BF16),
        scratch_shapes=[pltpu.VMEM((t + 2 * HALO, D_SHORT), F32)],
        compiler_params=_cparams(("parallel", "parallel"), 40),
        name="short_conv",
    )(p, p, p, p, p, p, p, conv_w)


def _rope(x, cos, sin):
    w = x.shape[1]
    lane = lax.broadcasted_iota(I32, x.shape, 1)
    swapped = jnp.where((lane & 31) < 16, pltpu.roll(x, w - 16, 1), pltpu.roll(x, 16, 1))
    return x * cos + swapped * sin


def _stack_heads(q, g):
    grp = N_Q_HEADS // N_KV_HEADS
    return jnp.concatenate(
        [q[:, (grp * g + i) * HEAD_DIM:(grp * g + i + 1) * HEAD_DIM] for i in range(grp)], axis=0)


def _softmax_pv(s, sink_col, v_ones, g):
    m = jnp.maximum(jnp.max(s, axis=1, keepdims=True), sink_col)
    o = _dot(jnp.exp2(s - m).astype(BF16), v_ones)
    den = o[:, LANES:LANES + 1] + jnp.exp2(sink_col - m)
    return o[:, g * HEAD_DIM:(g + 1) * HEAD_DIM] / den


def _with_ones(v):
    return jnp.concatenate([v, jnp.ones_like(v)], axis=1).astype(BF16)


def _unstack_heads(outs, nq):
    grp = N_Q_HEADS // N_KV_HEADS
    pieces = [o[i * nq:(i + 1) * nq] for o in outs for i in range(grp)]
    return jnp.concatenate(pieces, axis=1)


def _attn_kernel(q_ref, kv_ref, ckv_ref, cos_ref, sin_ref, bias_ref, sink_ref, o_ref, *, nb, qb):
    first = pl.program_id(1) * qb
    kvw = N_KV_HEADS * HEAD_DIM
    grp = N_Q_HEADS // N_KV_HEADS
    rep = N_Q_HEADS * HEAD_DIM // LANES

    q0 = pl.multiple_of(first * BLOCK, BLOCK)
    cq = cos_ref[pl.ds(q0, qb * BLOCK), :]
    sq = sin_ref[pl.ds(q0, qb * BLOCK), :]
    q = _rope(q_ref[...].astype(F32), jnp.concatenate([cq] * rep, axis=1),
              jnp.concatenate([sq] * rep, axis=1)) * (HEAD_DIM ** -0.5 * LOG2E)

    def kblock(j):
        start = pl.multiple_of(jnp.clip(first + j, 0, nb - 1) * BLOCK, BLOCK)
        kvb = kv_ref[pl.ds(start, BLOCK), :].astype(F32)
        k = _rope(kvb[:, :kvw], cos_ref[pl.ds(start, BLOCK), :], sin_ref[pl.ds(start, BLOCK), :])
        return k, kvb[:, kvw:].T

    blocks = [kblock(j) for j in range(-1, qb + 1)]
    ckv = ckv_ref[...].astype(F32)
    ctx_k = ckv[:, :kvw]
    ctx_vt = [ckv[i:i + BLOCK, kvw:].T for i in range(0, ckv.shape[0], BLOCK)]
    ones = jnp.ones((HEAD_DIM, 3 * BLOCK + ckv.shape[0]), F32)
    sinks = [sink_ref[g][:1, :] for g in range(N_KV_HEADS)]

    chains = [(i, g) for i in range(qb) for g in range(N_KV_HEADS)]
    scores = []
    for i, g in chains:
        bias_prev = bias_ref[:BLOCK, :] + jnp.where(first + i == 0, NEG_INF, 0.0)
        bias_next = bias_ref[BLOCK:, :] + jnp.where(first + i == nb - 1, NEG_INF, 0.0)
        qs = _stack_heads(q[i * BLOCK:(i + 1) * BLOCK], g).astype(BF16)
        k_all = jnp.concatenate([blocks[i][0], blocks[i + 1][0], blocks[i + 2][0], ctx_k], axis=0)
        kh = k_all[:, g * HEAD_DIM:(g + 1) * HEAD_DIM].astype(BF16)
        s = lax.dot_general(kh, qs, (((1,), (1,)), ((), ())), preferred_element_type=F32)
        scores.append(jnp.concatenate([s[:BLOCK] + bias_prev, s[BLOCK:2 * BLOCK],
                                       s[2 * BLOCK:3 * BLOCK] + bias_next, s[3 * BLOCK:]], axis=0))
    maxes = [jnp.maximum(jnp.max(s, axis=0, keepdims=True), sinks[g]) for s, (_, g) in zip(scores, chains)]
    exps = [jnp.exp2(s - m).astype(BF16) for s, m in zip(scores, maxes)]
    outs = [[] for _ in range(qb)]
    for (i, g), e, m in zip(chains, exps, maxes):
        v_t = jnp.concatenate([blocks[i][1], blocks[i + 1][1], blocks[i + 2][1]] + ctx_vt, axis=1)
        lhs = jnp.concatenate([v_t[g * HEAD_DIM:(g + 1) * HEAD_DIM], ones], axis=0).astype(BF16)
        o_t = _dot(lhs, e)
        den = o_t[HEAD_DIM:HEAD_DIM + 1] + jnp.exp2(sinks[g] - m)
        o_t = o_t * (1.0 / den)
        outs[i] += [o_t[:, h * BLOCK:(h + 1) * BLOCK].T[:, :HEAD_DIM] for h in range(grp)]
    for i in range(qb):
        o_ref[i * BLOCK:(i + 1) * BLOCK, :] = jnp.concatenate(outs[i], axis=1).astype(BF16)


def _band_bias():
    grp = N_Q_HEADS // N_KV_HEADS
    kj = jnp.arange(BLOCK)[:, None]
    qi = jnp.arange(BLOCK)[None, :]
    prev = jnp.where(qi <= kj, 0.0, NEG_INF)
    nxt = jnp.where(kj <= qi, 0.0, NEG_INF)
    return jnp.tile(jnp.concatenate([prev, nxt], axis=0).astype(F32), (1, grp))


def _sink_rows(sink):
    grp = N_Q_HEADS // N_KV_HEADS
    row = jnp.repeat(sink.astype(F32).reshape(N_KV_HEADS, grp) * LOG2E, BLOCK, axis=1)
    return jnp.broadcast_to(row[:, None, :], (N_KV_HEADS, 8, grp * BLOCK))


def _sink_cols(sink, nq):
    grp = N_Q_HEADS // N_KV_HEADS
    col = jnp.repeat(sink.astype(F32).reshape(N_KV_HEADS, grp) * LOG2E, nq, axis=1)
    return jnp.broadcast_to(col[:, :, None], (N_KV_HEADS, grp * nq, LANES))


def _latent_attention(p, pc, pc_kv_col, cos_t, sin_t, sink, batch, s, n_ctx):
    nb = s // BLOCK
    grp = N_Q_HEADS // N_KV_HEADS
    qw = N_Q_HEADS * HEAD_DIM
    kvw2 = 2 * N_KV_HEADS * HEAD_DIM
    bias = _band_bias()
    qb = 8 if nb % 8 == 0 else (4 if nb % 4 == 0 else 1)
    ns = nb // qb
    tab = pl.BlockSpec((s, LANES), lambda b, n: (0, 0))
    return pl.pallas_call(
        functools.partial(_attn_kernel, nb=nb, qb=qb),
        grid=(batch, ns),
        in_specs=[pl.BlockSpec((qb * BLOCK, qw), lambda b, n: (b * ns + n, COL_Q // qw)),
                  pl.BlockSpec((s, kvw2), lambda b, n: (b, COL_KV // kvw2)),
                  pl.BlockSpec((n_ctx, kvw2), lambda b, n: (b, pc_kv_col // kvw2)),
                  tab, tab,
                  pl.BlockSpec(bias.shape, lambda b, n: (0, 0)),
                  pl.BlockSpec((N_KV_HEADS, 8, grp * BLOCK), lambda b, n: (0, 0, 0))],
        out_specs=pl.BlockSpec((qb * BLOCK, qw), lambda b, n: (b * ns + n, 0)),
        out_shape=jax.ShapeDtypeStruct((batch * s, qw), BF16),
        compiler_params=_cparams(("parallel", "parallel"), 40),
        name="latent_attention",
    )(p, p, pc, cos_t, sin_t, bias, _sink_rows(sink))


def _cattn_kernel(q_ref, kv_ref, sink_ref, o_ref):
    nq = q_ref.shape[0]
    kvw = N_KV_HEADS * HEAD_DIM
    q = q_ref[...].astype(F32) * (HEAD_DIM ** -0.5 * LOG2E)
    kv = kv_ref[...].astype(F32)
    v_ones = _with_ones(kv[:, kvw:])
    outs = []
    for g in range(N_KV_HEADS):
        qs = _stack_heads(q, g).astype(BF16)
        kh = kv[:, g * HEAD_DIM:(g + 1) * HEAD_DIM].astype(BF16)
        s = lax.dot_general(qs, kh, (((1,), (1,)), ((), ())), preferred_element_type=F32)
        outs.append(_softmax_pv(s, sink_ref[g][:, :1], v_ones, g))
    o_ref[...] = _unstack_heads(outs, nq).astype(BF16)


def _context_attention(pc, sink, batch, n_ctx):
    grp = N_Q_HEADS // N_KV_HEADS
    qw = N_Q_HEADS * HEAD_DIM
    kvw2 = 2 * N_KV_HEADS * HEAD_DIM
    return pl.pallas_call(
        _cattn_kernel,
        grid=(batch,),
        in_specs=[pl.BlockSpec((n_ctx, qw), lambda b: (b, COL_Q // qw)),
                  pl.BlockSpec((n_ctx, kvw2), lambda b: (b, COL_KV // kvw2)),
                  pl.BlockSpec((N_KV_HEADS, grp * n_ctx, LANES), lambda b: (0, 0, 0))],
        out_specs=pl.BlockSpec((n_ctx, qw), lambda b: (b, 0)),
        out_shape=jax.ShapeDtypeStruct((batch * n_ctx, qw), BF16),
        compiler_params=_cparams(("parallel",), 40),
        name="context_attention",
    )(pc, pc, _sink_cols(sink, n_ctx))


FFT_TILE = 16


def _fft1_kernel(v_ref, w_ref, tc_ref, ts_ref, o_ref):
    df = D_FOURIER
    w = w_ref[...]
    n2 = w.shape[0] // 2
    for i in range(FFT_TILE):
        x = v_ref[:, i, :]
        z = jnp.concatenate([x[:, :df], x[:, df:]], axis=0).astype(BF16)
        c = _dot(w, z)
        cr, ci = c[:n2], c[n2:]
        tc = jnp.concatenate([tc_ref[i]] * (df // LANES), axis=1)
        ts = jnp.concatenate([ts_ref[i]] * (df // LANES), axis=1)
        o_ref[i, :, :df] = (cr * tc - ci * ts).astype(BF16)
        o_ref[i, :, df:] = (cr * ts + ci * tc).astype(BF16)


def _fft2_kernel(y_ref, w_ref, o_ref):
    df = D_FOURIER
    w = w_ref[...]
    for i in range(FFT_TILE):
        y = jnp.concatenate([y_ref[:, 2 * i * df:(2 * i + 1) * df],
                             y_ref[:, (2 * i + 1) * df:(2 * i + 2) * df]], axis=0)
        o_ref[:, i, :] = _dot(w, y)


def _fft_split(s):
    n1 = 1 << ((s.bit_length() - 1) // 2)
    return n1, s // n1


def _fft_tables(s):
    n1, n2 = _fft_split(s)
    c2, s2 = _dft_tables(n2)
    w1 = jnp.concatenate([jnp.concatenate([c2, -s2], axis=1),
                          jnp.concatenate([s2, c2], axis=1)], axis=0).astype(BF16)
    c1, s1 = _dft_tables(n1)
    w2 = jnp.concatenate([c1, -s1], axis=1).astype(BF16)
    ang = (jnp.arange(n1)[:, None] * jnp.arange(n2)[None, :]).astype(F32) * (2.0 * math.pi / s)
    tc = jnp.broadcast_to(jnp.cos(ang)[:, :, None], (n1, n2, LANES))
    ts = jnp.broadcast_to(jnp.sin(ang)[:, :, None], (n1, n2, LANES))
    return w1, w2, tc, ts


def _position_dft(v, tables, batch, s):
    w1, w2, tc, ts = tables
    n1, n2 = _fft_split(s)
    df2 = 2 * D_FOURIER
    nt1 = n1 // FFT_TILE
    nt2 = n2 // FFT_TILE
    stage1 = pl.pallas_call(
        _fft1_kernel,
        grid=(batch, nt1),
        in_specs=[pl.BlockSpec((n2, FFT_TILE, df2), lambda b, j: (b, j, 0)),
                  pl.BlockSpec(w1.shape, lambda b, j: (0, 0)),
                  pl.BlockSpec((FFT_TILE, n2, LANES), lambda b, j: (j, 0, 0)),
                  pl.BlockSpec((FFT_TILE, n2, LANES), lambda b, j: (j, 0, 0))],
        out_specs=pl.BlockSpec((FFT_TILE, n2, df2), lambda b, j: (b * nt1 + j, 0, 0)),
        out_shape=jax.ShapeDtypeStruct((batch * n1, n2, df2), BF16),
        compiler_params=_cparams(("parallel", "parallel"), 40),
        name="fft_stage1",
    )(v.reshape(batch * n2, n1, df2), w1, tc, ts)
    out = pl.pallas_call(
        _fft2_kernel,
        grid=(batch, nt2),
        in_specs=[pl.BlockSpec((n1, FFT_TILE * df2), lambda b, j: (b, j)),
                  pl.BlockSpec(w2.shape, lambda b, j: (0, 0))],
        out_specs=pl.BlockSpec((n1, FFT_TILE, D_FOURIER), lambda b, j: (b, j, 0)),
        out_shape=jax.ShapeDtypeStruct((batch * n1, n2, D_FOURIER), F32),
        compiler_params=_cparams(("parallel", "parallel"), 40),
        name="fft_stage2",
    )(stage1.reshape(batch * n1, n2 * df2), w2)
    return out.reshape(batch * s, D_FOURIER)


def _dft_tables(n):
    k = jnp.arange(n, dtype=I32)
    ang = ((k[:, None] * k[None, :]) % n).astype(F32) * (2.0 * math.pi / n)
    scale = n ** -0.5
    return jnp.cos(ang) * scale, jnp.sin(ang) * scale


def _channel_dft_matrix():
    cg = D_FOURIER // N_FOURIER_GROUPS
    c, s = _dft_tables(cg)
    eye = jnp.eye(N_FOURIER_GROUPS, dtype=F32)
    return jnp.concatenate([jnp.kron(eye, c), jnp.kron(eye, s)], axis=1).astype(BF16)


def _merge_kernel(ba_ref, bb_ref, bc_ref, bd_ref, gl_ref, gb_ref, wa_ref, wb_ref, wc_ref, wd_ref,
                  wo_ref, xs_ref, m2_ref, m3_ref, m4_ref, pg_ref, fg_ref, rw_ref,
                  xo_ref, hp_ref, aff_ref, aff_t_ref):
    tm, d = xs_ref.shape
    halves = [pl.ds(i * (tm // MERGE_PARTS), tm // MERGE_PARTS) for i in range(MERGE_PARTS)]
    ys = []
    for rs in halves:
        y = None
        for i, (b_ref, w_ref) in enumerate(((ba_ref, wa_ref), (bb_ref, wb_ref),
                                            (bc_ref, wc_ref), (bd_ref, wd_ref))):
            gate = _tanh_gate(gl_ref[rs, i * d:(i + 1) * d].astype(F32) + gb_ref[:, i * d:(i + 1) * d])
            term = gate * _dot(b_ref[rs, :].astype(BF16), w_ref[...])
            y = term if y is None else y + term
        ys.append(y.astype(BF16))
    zs = [_dot(y, wo_ref[...]) for y in ys]
    hs = []
    for rs, z in zip(halves, zs):
        xs = xs_ref[rs, :] + m2_ref[0] * _rms(z, pg_ref[...])
        xo_ref[rs, :] = xs
        h = _rms(xs, fg_ref[...]) * (1.0 + m4_ref[0]) + m3_ref[0]
        hp_ref[rs, :] = h.astype(BF16).astype(F32)
        hs.append(h)
    for rs, h in zip(halves, hs):
        h_hi, h_lo = _split_bf16(h)
        p_hi = _dot(h_hi, rw_ref[...])
        logits = p_hi + pltpu.roll(p_hi, LANES - N_EXPERTS, 1) + _dot(h_lo, rw_ref[...])
        lane = lax.broadcasted_iota(I32, logits.shape, 1)
        logits = jnp.where(lane < N_EXPERTS, logits, NEG_INF)
        e = jnp.exp(logits - jnp.max(logits, axis=1, keepdims=True))
        aff = e / jnp.sum(e, axis=1, keepdims=True)
        aff_ref[rs, :] = aff
        aff_t_ref[:, rs] = aff.T[:N_EXPERTS]


def _merge(ba, bb, bc, bd, p, gate_b, wa, wb, wc, wd, wo, xs, m2, m3, m4, post_g, ffn_g, rw,
           rows_per_group):
    r, d = xs.shape
    tm = min(512, rows_per_group)
    tiles_per_group = rows_per_group // tm
    gw = N_BRANCH * d

    def rows(width):
        return pl.BlockSpec((tm, width), lambda i: (i, 0))

    def const(shape):
        return pl.BlockSpec(shape, lambda i: (0,) * len(shape))

    mod = pl.BlockSpec((1, 1, d), lambda i: (i // tiles_per_group, 0, 0))
    half = d // 2
    return pl.pallas_call(
        _merge_kernel,
        grid=(r // tm,),
        in_specs=[rows(half), rows(half), rows(half), rows(half),
                  pl.BlockSpec((tm, gw), lambda i: (i, COL_G // gw)),
                  const((1, gw)),
                  const((half, d)), const((half, d)), const((half, d)), const((half, d)),
                  const((d, d)),
                  rows(d), mod, mod, mod, const((1, d)), const((1, d)), const((d, LANES))],
        out_specs=[rows(d), rows(d), rows(LANES),
                   pl.BlockSpec((N_EXPERTS, tm), lambda i: (0, i))],
        out_shape=[jax.ShapeDtypeStruct((r, d), F32),
                   jax.ShapeDtypeStruct((r, d), F32),
                   jax.ShapeDtypeStruct((r, LANES), F32),
                   jax.ShapeDtypeStruct((N_EXPERTS, r), F32)],
        compiler_params=_cparams(("parallel",), 48),
        name="merge_router",
    )(ba, bb, bc, bd, p, 0.5 * gate_b.reshape(1, gw), wa, wb, wc, wd, wo, xs, m2, m3, m4,
      post_g.reshape(1, d), ffn_g.reshape(1, d), rw)


def _route_kernel(aff_ref, aff_t_ref, tri_ref, tl_ref, idx_ref, val_ref, *, cap):
    s = aff_ref.shape[0]
    nslot = idx_ref.shape[-1]
    aff = aff_ref[...]

    def as_float(bits):
        return lax.bitcast_convert_type(bits, F32)

    def count(mask):
        part = jnp.sum(jnp.where(mask, 1.0, 0.0).reshape(s // 64, 64, LANES), axis=0)
        return jnp.sum(part, axis=0, keepdims=True)

    aff_t = aff_t_ref[...]

    def search(i, thr):
        cand = thr | lax.shift_left(jnp.int32(1), 30 - i)
        above = jnp.sum(jnp.where(aff_t >= as_float(cand), 1.0, 0.0), axis=1, keepdims=True)
        return jnp.where(above >= cap, cand, thr)

    thr = lax.fori_loop(0, 31, search, jnp.zeros((N_EXPERTS, 1), I32))

    def as_lane_row(col):
        block = jnp.concatenate([jnp.broadcast_to(col, (N_EXPERTS, LANES)),
                                 jnp.zeros((LANES - N_EXPERTS, LANES), F32)], axis=0)
        return block.T[0:1, :]

    gt = aff >= as_lane_row(as_float(jnp.maximum(thr + 1, MIN_NORMAL_BITS)))
    eq = (aff >= as_lane_row(as_float(thr))) & jnp.logical_not(gt)
    need = cap - count(gt)

    tri = tri_ref[...]

    def cumsum_excl(m):
        off = jnp.zeros((1, LANES), F32)
        outs = []
        for c in range(s // LANES):
            mc = m[c * LANES:(c + 1) * LANES]
            cs = _dot(tri, mc.astype(BF16))
            outs.append(cs - mc + off)
            off = off + cs[LANES - 1:LANES, :]
        return jnp.concatenate(outs, axis=0)

    eq_f = jnp.where(eq, 1.0, 0.0)
    sel = gt | (eq & (cumsum_excl(eq_f) < need))
    sel_f = jnp.where(sel, 1.0, 0.0)
    pos = jnp.where(sel, cumsum_excl(sel_f), -1.0)

    slot = lax.broadcasted_iota(I32, (s, nslot), 1).astype(F32)
    tl = tl_ref[...]
    row = lax.broadcasted_iota(I32, tl.shape, 0)
    vals = []
    for e in range(N_EXPERTS):
        onehot = jnp.where(pos[:, e:e + 1] == slot, 1.0, 0.0).astype(BF16)
        a = aff_t_ref[e:e + 1, :]
        a_hi = a.astype(BF16).astype(F32)
        a_mid = (a - a_hi).astype(BF16).astype(F32)
        a_lo = a - a_hi - a_mid
        lhs = jnp.where(row == 2, a_hi, jnp.where(row == 3, a_mid, jnp.where(row == 4, a_lo, tl)))
        res = _dot(lhs.astype(BF16), onehot)
        idx_ref[0, e:e + 1, :] = (res[0:1] * 64.0 + res[1:2] + 0.5).astype(I32)
        vals.append(res[2:3] + res[3:4] + res[4:5])
    vals = jnp.concatenate(vals + [jnp.zeros((LANES - N_EXPERTS, nslot), F32)], axis=0)
    vals_t = jnp.concatenate([vals[:, c * LANES:(c + 1) * LANES].T for c in range(nslot // LANES)],
                             axis=0)
    val_ref[...] = vals_t[:cap]


def _route(aff, aff_t, batch, s, cap):
    nslot = max(cap, LANES)
    tri = (jnp.arange(LANES)[:, None] >= jnp.arange(LANES)[None, :]).astype(BF16)
    t = jnp.arange(s)
    tl = jnp.zeros((8, s), F32).at[0].set(t // 64).at[1].set(t % 64)
    idx, vals = pl.pallas_call(
        functools.partial(_route_kernel, cap=cap),
        grid=(batch,),
        in_specs=[pl.BlockSpec((s, LANES), lambda b: (b, 0)),
                  pl.BlockSpec((N_EXPERTS, s), lambda b: (0, b)),
                  pl.BlockSpec((LANES, LANES), lambda b: (0, 0)),
                  pl.BlockSpec((8, s), lambda b: (0, 0))],
        out_specs=[pl.BlockSpec((1, N_EXPERTS, nslot), lambda b: (b, 0, 0)),
                   pl.BlockSpec((cap, LANES), lambda b: (b, 0))],
        out_shape=[jax.ShapeDtypeStruct((batch, N_EXPERTS, nslot), I32),
                   jax.ShapeDtypeStruct((batch * cap, LANES), F32)],
        compiler_params=_cparams(("parallel",), 48),
        name="route_topk",
    )(aff, aff_t, tri, tl)
    return idx[:, :, :cap].reshape(-1), vals


def _tile_row(t):
    return lax.shift_right_logical(t, SUBLANES.bit_length() - 1), t & (SUBLANES - 1)


def _experts_per_step(cap):
    return max(1, min(N_EXPERTS, 512 // cap))


def _gather_kernel(idx_ref, h_ref, xg_ref, g_ref, *, cap, eps):
    b = pl.program_id(0)
    first = pl.program_id(1) * eps
    groups = cap // SUBLANES

    for ei in range(eps):
        base = (b * N_EXPERTS + first + ei) * cap

        def body(jg, carry, ei=ei, base=base):
            j0 = pl.multiple_of(jg * SUBLANES, SUBLANES)
            for k in range(SUBLANES):
                t = idx_ref[base + j0 + k]
                hi, lo = _tile_row(t)
                g_ref[ei * groups + jg, pl.ds(k, 1), :] = h_ref[hi, pl.ds(lo, 1), :]
            return carry

        lax.fori_loop(0, groups, body, 0)
    xg_ref[...] = g_ref[...].reshape(xg_ref.shape).astype(BF16)


def _gather(idx, hp, batch, s, cap):
    d = hp.shape[1]
    eps = _experts_per_step(cap)
    grid_spec = pltpu.PrefetchScalarGridSpec(
        num_scalar_prefetch=1,
        grid=(batch, N_EXPERTS // eps),
        in_specs=[pl.BlockSpec((s // SUBLANES, SUBLANES, d), lambda b, e, idx: (b, 0, 0))],
        out_specs=pl.BlockSpec((eps, cap, d), lambda b, e, idx: (e, b, 0)),
        scratch_shapes=[pltpu.VMEM((eps * cap // SUBLANES, SUBLANES, d), F32)])
    xg = pl.pallas_call(
        functools.partial(_gather_kernel, cap=cap, eps=eps),
        grid_spec=grid_spec,
        out_shape=jax.ShapeDtypeStruct((N_EXPERTS, batch * cap, d), BF16),
        compiler_params=_cparams(("arbitrary", "arbitrary"), 48),
        name="moe_gather",
    )(idx, hp.reshape(-1, SUBLANES, d))
    return xg.reshape(N_EXPERTS * batch * cap, d)


def _ffn_kernel(*refs, n_sets):
    ins, rest = refs[:2 * n_sets], refs[2 * n_sets:]
    w1_ref, w3_ref, w2_ref = rest[:3]
    outs = rest[3:3 + n_sets]
    w1b, w3b, w2b = rest[3 + n_sets:]
    expert = pl.program_id(0)

    def run(x_ref, v_ref, y_ref):
        x = x_ref[...]
        hid = _silu(_dot(x, w1b[...])) * _dot(x, w3b[...])
        y = _dot(hid.astype(BF16), w2b[...])
        lane = lax.broadcasted_iota(I32, v_ref.shape, 1)
        v = jnp.sum(jnp.where(lane == expert, v_ref[...], 0.0), axis=1, keepdims=True)
        y_ref[...] = y * v

    @pl.when(pl.program_id(1) == 0)
    def _():
        w1b[...] = w1_ref[0, 0].astype(BF16)
        w3b[...] = w3_ref[0, 0].astype(BF16)
        w2b[...] = w2_ref[0, 0].astype(BF16)
        for k in range(1, n_sets):
            run(ins[2 * k], ins[2 * k + 1], outs[k])

    run(ins[0], ins[1], outs[0])


def _ffn(sets, w1, w3, w2, layer):
    d = sets[0][0].shape[1]
    f = w1.shape[-1]
    tm = min(1024, sets[0][2])
    nt = sets[0][2] // tm

    def wspec(a, c):
        return pl.BlockSpec((1, 1, a, c), lambda e, m: (layer, e, 0, 0))

    in_specs = [pl.BlockSpec((tm, d), lambda e, m: (e * nt + m, 0)),
                pl.BlockSpec((tm, LANES), lambda e, m: (m, 0))]
    out_specs = [pl.BlockSpec((tm, d), lambda e, m: (e * nt + m, 0))]
    operands = [sets[0][0], sets[0][1]]
    for xg, vals, rpe in sets[1:]:
        in_specs += [pl.BlockSpec((rpe, d), lambda e, m: (e, 0)),
                     pl.BlockSpec((rpe, LANES), lambda e, m: (0, 0))]
        out_specs.append(pl.BlockSpec((rpe, d), lambda e, m: (e, 0)))
        operands += [xg, vals]
    return pl.pallas_call(
        functools.partial(_ffn_kernel, n_sets=len(sets)),
        grid=(N_EXPERTS, nt),
        in_specs=in_specs + [wspec(d, f), wspec(d, f), wspec(f, d)],
        out_specs=out_specs,
        out_shape=[jax.ShapeDtypeStruct(xg.shape, F32) for xg, _, _ in sets],
        scratch_shapes=[pltpu.VMEM((d, f), BF16), pltpu.VMEM((d, f), BF16), pltpu.VMEM((f, d), BF16)],
        compiler_params=_cparams(("parallel", "arbitrary"), 56),
        name="expert_ffn",
    )(*operands, w1, w3, w2)


def _combine_kernel(idx_ref, y_ref, xs_ref, m5_ref, g_ref, o_ref, acc_ref, *, cap, tf, eps):
    b = pl.program_id(0)
    step = pl.program_id(1)
    scatter_steps = N_EXPERTS // eps

    @pl.when(step == 0)
    def _():
        acc_ref[...] = jnp.zeros_like(acc_ref)

    @pl.when(step < scatter_steps)
    def _():
        for ei in range(eps):
            base = (b * N_EXPERTS + step * eps + ei) * cap

            def body(jg, carry, ei=ei, base=base):
                j0 = pl.multiple_of(jg * SUBLANES, SUBLANES)
                toks = [_tile_row(idx_ref[base + j0 + k]) for k in range(SUBLANES)]
                rows = [acc_ref[hi, pl.ds(lo, 1), :] for hi, lo in toks]
                for k, (hi, lo) in enumerate(toks):
                    acc_ref[hi, pl.ds(lo, 1), :] = rows[k] + y_ref[ei, jg, pl.ds(k, 1), :]
                return carry

            lax.fori_loop(0, cap // SUBLANES, body, 0)

    @pl.when(step >= scatter_steps)
    def _():
        r0 = pl.multiple_of((step - scatter_steps) * (tf // SUBLANES), tf // SUBLANES)
        moe = acc_ref[pl.ds(r0, tf // SUBLANES)].reshape(o_ref.shape)
        o_ref[...] = xs_ref[...] + m5_ref[0] * _rms(moe, g_ref[...])


def _combine(idx, y, xs, m5, post_g, batch, s, cap, shared_mod):
    d = xs.shape[1]
    tf = min(512, s)
    nfin = s // tf
    eps = _experts_per_step(cap)
    scatter_steps = N_EXPERTS // eps

    def chunk_map(b, st, idx):
        return (b * nfin + jnp.maximum(st - scatter_steps, 0), 0)

    grid_spec = pltpu.PrefetchScalarGridSpec(
        num_scalar_prefetch=1,
        grid=(batch, scatter_steps + nfin),
        in_specs=[pl.BlockSpec((eps, cap // SUBLANES, SUBLANES, d),
                               lambda b, st, idx: (jnp.minimum(st, scatter_steps - 1), b, 0, 0)),
                  pl.BlockSpec((tf, d), chunk_map),
                  pl.BlockSpec((1, 1, d), lambda b, st, idx: (0 if shared_mod else b, 0, 0)),
                  pl.BlockSpec((1, d), lambda b, st, idx: (0, 0))],
        out_specs=pl.BlockSpec((tf, d), chunk_map),
        scratch_shapes=[pltpu.VMEM((s // SUBLANES, SUBLANES, d), F32)])
    return pl.pallas_call(
        functools.partial(_combine_kernel, cap=cap, tf=tf, eps=eps),
        grid_spec=grid_spec,
        out_shape=jax.ShapeDtypeStruct(xs.shape, F32),
        compiler_params=_cparams(("arbitrary", "arbitrary"), 48),
        name="moe_combine",
    )(idx, y.reshape(N_EXPERTS, -1, SUBLANES, d), xs, m5, post_g.reshape(1, d))


def _prep_w_in(w):
    d = w.shape[0]
    kv = 2 * N_KV_HEADS * HEAD_DIM
    o_q = 2 * D_CONV + 3 * D_SHORT
    o_k = o_q + N_Q_HEADS * HEAD_DIM
    o_f = o_k + kv
    o_g = o_f + D_FOURIER
    parts = [w[:, :o_k], w[:, o_f:o_g], w[:, o_k:o_f],
             jnp.zeros((d, COL_G - COL_KV - kv), w.dtype), 0.5 * w[:, o_g:]]
    return jnp.concatenate(parts, axis=1).astype(BF16)


def _rope_tables(s):
    t = jnp.arange(s)
    row = (t // GRID_W).astype(F32)
    col = (t % GRID_W).astype(F32)
    nf = HEAD_DIM // 4
    inv = ROPE_BASE ** (-jnp.arange(nf, dtype=F32) / nf)
    ar = row[:, None] * inv
    ac = col[:, None] * inv
    cos = jnp.concatenate([jnp.cos(ar), jnp.cos(ar), jnp.cos(ac), jnp.cos(ac)], axis=1)
    sin = jnp.concatenate([-jnp.sin(ar), jnp.sin(ar), -jnp.sin(ac), jnp.sin(ac)], axis=1)
    rep = LANES // HEAD_DIM
    return jnp.tile(cos, (1, rep)), jnp.tile(sin, (1, rep))


def _dispatch(routed, batch, s):
    hp, aff, aff_t = routed
    cap = CAPACITY_FACTOR * s // N_EXPERTS
    idx, vals = _route(aff, aff_t, batch, s, cap)
    return idx, (_gather(idx, hp, batch, s, cap), vals, batch * cap)


def kernel(x, c, ctx, c_ctx, ada_w, ada_b, pre_mix_g, post_mix_g, pre_ffn_g, post_ffn_g, w_in, gate_b, conv_a_w, conv_a_b, ln_a_g, ln_a_b, w_a_out, conv_b_w, w_b_out, sink, w_c_out, w_d_out, w_o, router_w, exp_w1, exp_w3, exp_w2):
    batch, s, d = x.shape
    n_ctx = ctx.shape[1]
    depth = ada_w.shape[0]

    cvec = jnp.zeros((16, d), F32).at[:batch].set(c).at[batch].set(c_ctx)
    mod = _ada(cvec, ada_w, ada_b)
    cos_t, sin_t = _rope_tables(s)
    bd = _channel_dft_matrix()
    dft_x = _fft_tables(s)
    dft_c = _fft_tables(n_ctx)

    xs = x.reshape(batch * s, d)
    cs = ctx.reshape(batch * n_ctx, d)
    for l in range(depth):
        last = l == depth - 1
        mx = [mod[l, :batch, k * d:(k + 1) * d].reshape(batch, 1, d) for k in range(6)]
        mc = [mod[l, batch:batch + 1, k * d:(k + 1) * d].reshape(1, 1, d) for k in range(6)]
        g_pre = pre_mix_g[l].reshape(1, d)
        w = _prep_w_in(w_in[l])
        wa, wb, wc, wd = ((0.5 * t[l]).astype(BF16) for t in (w_a_out, w_b_out, w_c_out, w_d_out))
        wo = w_o[l].astype(BF16)
        r_hi = router_w[l].astype(BF16)
        r_lo = (router_w[l] - r_hi.astype(F32)).astype(BF16)
        rw = (jnp.zeros((d, LANES), BF16).at[:, :N_EXPERTS].set(r_hi)
              .at[:, N_EXPERTS:2 * N_EXPERTS].set(r_lo))

        p = _inproj(xs, mx[0], mx[1], g_pre, w, s)
        if last:
            kv_tile = COL_KV // INPROJ_TN * INPROJ_TN
            pc = _inproj(cs, mc[0], mc[1], g_pre, w[:, kv_tile:kv_tile + INPROJ_TN], batch * n_ctx)
            pc_kv_col = COL_KV - kv_tile
        else:
            pc = _inproj(cs, mc[0], mc[1], g_pre, w, batch * n_ctx)
            pc_kv_col = COL_KV

        def mixer(pp, att, seq, tables, xres, m, rows_per_group):
            ba = _conformer(pp, conv_a_w[l], conv_a_b[l], ln_a_g[l], ln_a_b[l], batch, seq)
            bb = _short_conv(pp, conv_b_w[l], batch, seq)
            v = _mm(pp, bd, a_cols=COL_F, out_dtype=F32)
            bf = _position_dft(v, tables, batch, seq)
            return _merge(ba, bb, att, bf, pp, gate_b[l], wa, wb, wc, wd, wo, xres,
                          m[2], m[3], m[4], post_mix_g[l], pre_ffn_g[l], rw, rows_per_group)

        att_x = _latent_attention(p, pc, pc_kv_col, cos_t, sin_t, sink[l], batch, s, n_ctx)
        xs, *routed = mixer(p, att_x, s, dft_x, xs, mx, s)
        idx_x, set_x = _dispatch(routed, batch, s)
        cap_x = CAPACITY_FACTOR * s // N_EXPERTS
        if last:
            y_x, = _ffn([set_x], exp_w1, exp_w3, exp_w2, l)
        else:
            att_c = _context_attention(pc, sink[l], batch, n_ctx)
            cs, *routed_c = mixer(pc, att_c, n_ctx, dft_c, cs, mc, batch * n_ctx)
            idx_c, set_c = _dispatch(routed_c, batch, n_ctx)
            y_x, y_c = _ffn([set_x, set_c], exp_w1, exp_w3, exp_w2, l)
            cs = _combine(idx_c, y_c, cs, mc[5], post_ffn_g[l], batch, n_ctx,
                          CAPACITY_FACTOR * n_ctx // N_EXPERTS, True)
        xs = _combine(idx_x, y_x, xs, mx[5], post_ffn_g[l], batch, s, cap_x, False)
    return xs.reshape(batch, s, d)
```

```python
import functools
import math

import jax
import jax.numpy as jnp
from jax import lax
from jax.experimental import pallas as pl
from jax.experimental.pallas import tpu as pltpu

F32 = jnp.float32
BF16 = jnp.bfloat16
I32 = jnp.int32

D_MODEL = 1024
GRID_W = 64
D_CONV = 512
CONV_K = 31
D_SHORT = 512
SHORT_K = 3
N_Q_HEADS = 8
N_KV_HEADS = 2
HEAD_DIM = 64
BLOCK = 128
D_FOURIER = 512
N_FOURIER_GROUPS = 4
N_BRANCH = 4
N_EXPERTS = 16
CAPACITY_FACTOR = 2
D_EXPERT = 1024
ROPE_BASE = 10000.0
EPS = 1e-6
NEG_INF = -1e30
LOG2E = 1.4426950408889634
MIN_NORMAL_BITS = 0x00800000

LANES = 128
SUBLANES = 8
HALO = 16
CONV_CHUNK = 256

COL_A = 0
COL_SHORT = 1024
COL_Q = 2560
COL_F = 3072
COL_KV = 3584
COL_G = 4096
N_PROJ = 8192
INPROJ_TN = 2048
MERGE_PARTS = 2


def _cparams(sem, vmem_mb):
    return pltpu.CompilerParams(dimension_semantics=sem,
                                vmem_limit_bytes=vmem_mb * 1024 * 1024)


def _sigmoid(x):
    return 0.5 * jnp.tanh(0.5 * x) + 0.5


def _tanh_gate(half_x):
    return jnp.tanh(half_x) + 1.0


def _silu(x):
    return x * _sigmoid(x)


def _rms(x, g):
    return x * lax.rsqrt(jnp.mean(x * x, axis=-1, keepdims=True) + EPS) * g


def _split_bf16(x):
    hi = x.astype(BF16)
    lo = (x - hi.astype(F32)).astype(BF16)
    return hi, lo


def _dot(a, b):
    return jnp.dot(a, b, preferred_element_type=F32)


def _dot3(a, b):
    ah, al = _split_bf16(a)
    bh, bl = _split_bf16(b)
    return _dot(ah, bh) + _dot(ah, bl) + _dot(al, bh)


def _ada_kernel(c_ref, w_ref, b_ref, o_ref):
    c = c_ref[...]
    o_ref[0] = _dot3(_silu(c), w_ref[0]) + b_ref[0]


def _ada(cvec, ada_w, ada_b):
    nl, d, n6 = ada_w.shape
    rows = cvec.shape[0]
    return pl.pallas_call(
        _ada_kernel,
        grid=(nl, n6 // d),
        in_specs=[pl.BlockSpec((rows, d), lambda l, j: (0, 0)),
                  pl.BlockSpec((1, d, d), lambda l, j: (l, 0, j)),
                  pl.BlockSpec((1, 1, d), lambda l, j: (l, 0, j))],
        out_specs=pl.BlockSpec((1, rows, d), lambda l, j: (l, 0, j)),
        out_shape=jax.ShapeDtypeStruct((nl, rows, n6), F32),
        compiler_params=_cparams(("parallel", "parallel"), 40),
        name="ada_mod",
    )(cvec, ada_w, ada_b.reshape(nl, 1, n6))


def _inproj_kernel(x_ref, sh_ref, sc_ref, g_ref, w_ref, o_ref, h_ref):
    @pl.when(pl.program_id(1) == 0)
    def _():
        y = _rms(x_ref[...], g_ref[...])
        h_ref[...] = (y * (1.0 + sc_ref[0]) + sh_ref[0]).astype(BF16)

    o_ref[...] = _dot(h_ref[...], w_ref[...]).astype(BF16)


def _inproj(x2d, shift, scale, gain, w, rows_per_group):
    r, d = x2d.shape
    n = w.shape[1]
    tm = min(2048, rows_per_group // 2)
    tn = INPROJ_TN
    tiles_per_group = rows_per_group // tm
    mod_spec = pl.BlockSpec((1, 1, d), lambda i, j: (i // tiles_per_group, 0, 0))
    return pl.pallas_call(
        _inproj_kernel,
        grid=(r // tm, n // tn),
        in_specs=[pl.BlockSpec((tm, d), lambda i, j: (i, 0)),
                  mod_spec, mod_spec,
                  pl.BlockSpec((1, d), lambda i, j: (0, 0)),
                  pl.BlockSpec((d, tn), lambda i, j: (0, j))],
        out_specs=pl.BlockSpec((tm, tn), lambda i, j: (i, j)),
        out_shape=jax.ShapeDtypeStruct((r, n), BF16),
        scratch_shapes=[pltpu.VMEM((tm, d), BF16)],
        compiler_params=_cparams(("parallel", "arbitrary"), 56),
        name="inproj",
    )(x2d, shift, scale, gain, w)


def _mm_kernel(a_ref, b_ref, o_ref):
    o_ref[...] = _dot(a_ref[...], b_ref[...]).astype(o_ref.dtype)


def _mm(a, b, *, a_cols=None, out_dtype=BF16, tm=1024):
    k, n = b.shape
    r = a.shape[0]
    cb = 0 if a_cols is None else a_cols // k
    tm = min(tm, r)
    return pl.pallas_call(
        _mm_kernel,
        grid=(r // tm,),
        in_specs=[pl.BlockSpec((tm, k), lambda i: (i, cb)),
                  pl.BlockSpec((k, n), lambda i: (0, 0))],
        out_specs=pl.BlockSpec((tm, n), lambda i: (i, 0)),
        out_shape=jax.ShapeDtypeStruct((r, n), out_dtype),
        compiler_params=_cparams(("parallel",), 40),
        name="matmul",
    )(a, b)


def _fill_window(win_ref, cur, prev, nxt, t):
    n = pl.program_id(1)
    last = pl.num_programs(1) - 1
    win_ref[HALO:HALO + t, :] = cur
    win_ref[0:HALO, :] = jnp.where(n > 0, prev, 0.0)
    win_ref[HALO + t:HALO + t + HALO, :] = jnp.where(n < last, nxt, 0.0)


def _dwconv_chunk(win_ref, t0, w_ref, ktaps):
    off = HALO - ktaps // 2
    nfull = -(-(off + ktaps) // 8) * 8
    w = win_ref[pl.ds(t0, CONV_CHUNK + nfull), :]
    acc = None
    for r in range(8):
        part = None
        for a in range(nfull // 8):
            j = 8 * a + r - off
            if 0 <= j < ktaps:
                term = w[8 * a:8 * a + CONV_CHUNK + 8] * w_ref[j:j + 1, :]
                part = term if part is None else part + term
        if part is not None:
            shifted = part[r:r + CONV_CHUNK]
            acc = shifted if acc is None else acc + shifted
    return acc


def _conformer_kernel(cur_ref, prev_ref, next_ref, cw_ref, cb_ref, lg_ref, lb_ref,
                      o_ref, win_ref, *, t):
    dc = o_ref.shape[-1]

    def glu(ref):
        blk = ref[...].astype(F32)
        return blk[:, :dc] * _sigmoid(blk[:, dc:])

    _fill_window(win_ref, glu(cur_ref), glu(prev_ref), glu(next_ref), t)

    def chunk(i, carry):
        t0 = pl.multiple_of(i * CONV_CHUNK, CONV_CHUNK)
        h = _dwconv_chunk(win_ref, t0, cw_ref, CONV_K) + cb_ref[...]
        mu = jnp.mean(h, axis=-1, keepdims=True)
        hc = h - mu
        y = hc * lax.rsqrt(jnp.mean(hc * hc, axis=-1, keepdims=True) + EPS)
        y = y * lg_ref[...] + lb_ref[...]
        o_ref[pl.ds(t0, CONV_CHUNK), :] = _silu(y).astype(BF16)
        return carry

    lax.fori_loop(0, t // CONV_CHUNK, chunk, 0)


def _halo_specs(width, col_block, s, t):
    nblk = t // HALO
    per_sample = s // HALO

    def prev_map(b, n):
        return (jnp.maximum(b * per_sample + n * nblk - 1, 0), col_block)

    def next_map(b, n):
        return (jnp.minimum(b * per_sample + (n + 1) * nblk, (b + 1) * per_sample - 1), col_block)

    return pl.BlockSpec((HALO, width), prev_map), pl.BlockSpec((HALO, width), next_map)


def _conformer(p, conv_w, conv_b, ln_g, ln_b, batch, s):
    t = min(1024, s)
    nt = s // t
    prev_spec, next_spec = _halo_specs(2 * D_CONV, COL_A // (2 * D_CONV), s, t)
    vec = pl.BlockSpec((1, D_CONV), lambda b, n: (0, 0))
    return pl.pallas_call(
        functools.partial(_conformer_kernel, t=t),
        grid=(batch, nt),
        in_specs=[pl.BlockSpec((t, 2 * D_CONV), lambda b, n: (b * nt + n, COL_A // (2 * D_CONV))),
                  prev_spec, next_spec,
                  pl.BlockSpec((CONV_K, D_CONV), lambda b, n: (0, 0)),
                  vec, vec, vec],
        out_specs=pl.BlockSpec((t, D_CONV), lambda b, n: (b * nt + n, 0)),
        out_shape=jax.ShapeDtypeStruct((batch * s, D_CONV), BF16),
        scratch_shapes=[pltpu.VMEM((t + 2 * HALO, D_CONV), F32)],
        compiler_params=_cparams(("parallel", "parallel"), 40),
        name="conformer_conv",
    )(p, p, p, conv_w, conv_b.reshape(1, -1), ln_g.reshape(1, -1), ln_b.reshape(1, -1))


def _short_kernel(bg_ref, cg_ref, hv_ref, cgp_ref, hvp_ref, cgn_ref, hvn_ref, w_ref,
                  o_ref, win_ref, *, t):
    def prod(a_ref, b_ref):
        return a_ref[...].astype(F32) * b_ref[...].astype(F32)

    _fill_window(win_ref, prod(cg_ref, hv_ref), prod(cgp_ref, hvp_ref), prod(cgn_ref, hvn_ref), t)

    def chunk(i, carry):
        t0 = pl.multiple_of(i * CONV_CHUNK, CONV_CHUNK)
        h = _dwconv_chunk(win_ref, t0, w_ref, SHORT_K)
        bg = bg_ref[pl.ds(t0, CONV_CHUNK), :].astype(F32)
        o_ref[pl.ds(t0, CONV_CHUNK), :] = (bg * h).astype(BF16)
        return carry

    lax.fori_loop(0, t // CONV_CHUNK, chunk, 0)


def _short_conv(p, conv_w, batch, s):
    t = min(1024, s)
    nt = s // t
    cb = COL_SHORT // D_SHORT
    cgp, cgn = _halo_specs(D_SHORT, cb + 1, s, t)
    hvp, hvn = _halo_specs(D_SHORT, cb + 2, s, t)

    def cur(k):
        return pl.BlockSpec((t, D_SHORT), lambda b, n: (b * nt + n, cb + k))

    return pl.pallas_call(
        functools.partial(_short_kernel, t=t),
        grid=(batch, nt),
        in_specs=[cur(0), cur(1), cur(2), cgp, hvp, cgn, hvn,
                  pl.BlockSpec((SHORT_K, D_SHORT), lambda b, n: (0, 0))],
        out_specs=pl.BlockSpec((t, D_SHORT), lambda b, n: (b * nt + n, 0)),
        out_shape=jax.ShapeDtypeStruct((batch * s, D_SHORT), BF16),
        scratch_shapes=[pltpu.VMEM((t + 2 * HALO, D_SHORT), F32)],
        compiler_params=_cparams(("parallel", "parallel"), 40),
        name="short_conv",
    )(p, p, p, p, p, p, p, conv_w)


def _rope(x, cos, sin):
    w = x.shape[1]
    lane = lax.broadcasted_iota(I32, x.shape, 1)
    swapped = jnp.where((lane & 31) < 16, pltpu.roll(x, w - 16, 1), pltpu.roll(x, 16, 1))
    return x * cos + swapped * sin


def _stack_heads(q, g):
    grp = N_Q_HEADS // N_KV_HEADS
    return jnp.concatenate(
        [q[:, (grp * g + i) * HEAD_DIM:(grp * g + i + 1) * HEAD_DIM] for i in range(grp)], axis=0)


def _softmax_pv(s, sink_col, v_ones, g):
    m = jnp.maximum(jnp.max(s, axis=1, keepdims=True), sink_col)
    o = _dot(jnp.exp2(s - m).astype(BF16), v_ones)
    den = o[:, LANES:LANES + 1] + jnp.exp2(sink_col - m)
    return o[:, g * HEAD_DIM:(g + 1) * HEAD_DIM] / den


def _with_ones(v):
    return jnp.concatenate([v, jnp.ones_like(v)], axis=1).astype(BF16)


def _unstack_heads(outs, nq):
    grp = N_Q_HEADS // N_KV_HEADS
    pieces = [o[i * nq:(i + 1) * nq] for o in outs for i in range(grp)]
    return jnp.concatenate(pieces, axis=1)


def _attn_kernel(q_ref, kv_ref, ckv_ref, cos_ref, sin_ref, bias_ref, sink_ref, o_ref, *, nb, qb):
    first = pl.program_id(1) * qb
    kvw = N_KV_HEADS * HEAD_DIM
    grp = N_Q_HEADS // N_KV_HEADS
    rep = N_Q_HEADS * HEAD_DIM // LANES

    q0 = pl.multiple_of(first * BLOCK, BLOCK)
    cq = cos_ref[pl.ds(q0, qb * BLOCK), :]
    sq = sin_ref[pl.ds(q0, qb * BLOCK), :]
    q = _rope(q_ref[...].astype(F32), jnp.concatenate([cq] * rep, axis=1),
              jnp.concatenate([sq] * rep, axis=1)) * (HEAD_DIM ** -0.5 * LOG2E)

    def kblock(j):
        start = pl.multiple_of(jnp.clip(first + j, 0, nb - 1) * BLOCK, BLOCK)
        kvb = kv_ref[pl.ds(start, BLOCK), :].astype(F32)
        k = _rope(kvb[:, :kvw], cos_ref[pl.ds(start, BLOCK), :], sin_ref[pl.ds(start, BLOCK), :])
        return k, kvb[:, kvw:].T

    blocks = [kblock(j) for j in range(-1, qb + 1)]
    ckv = ckv_ref[...].astype(F32)
    ctx_k = ckv[:, :kvw]
    ctx_vt = [ckv[i:i + BLOCK, kvw:].T for i in range(0, ckv.shape[0], BLOCK)]
    ones = jnp.ones((HEAD_DIM, 3 * BLOCK + ckv.shape[0]), F32)
    sinks = [sink_ref[g][:1, :] for g in range(N_KV_HEADS)]

    chains = [(i, g) for i in range(qb) for g in range(N_KV_HEADS)]
    scores = []
    for i, g in chains:
        bias_prev = bias_ref[:BLOCK, :] + jnp.where(first + i == 0, NEG_INF, 0.0)
        bias_next = bias_ref[BLOCK:, :] + jnp.where(first + i == nb - 1, NEG_INF, 0.0)
        qs = _stack_heads(q[i * BLOCK:(i + 1) * BLOCK], g).astype(BF16)
        k_all = jnp.concatenate([blocks[i][0], blocks[i + 1][0], blocks[i + 2][0], ctx_k], axis=0)
        kh = k_all[:, g * HEAD_DIM:(g + 1) * HEAD_DIM].astype(BF16)
        s = lax.dot_general(kh, qs, (((1,), (1,)), ((), ())), preferred_element_type=F32)
        scores.append(jnp.concatenate([s[:BLOCK] + bias_prev, s[BLOCK:2 * BLOCK],
                                       s[2 * BLOCK:3 * BLOCK] + bias_next, s[3 * BLOCK:]], axis=0))
    maxes = [jnp.maximum(jnp.max(s, axis=0, keepdims=True), sinks[g]) for s, (_, g) in zip(scores, chains)]
    exps = [jnp.exp2(s - m).astype(BF16) for s, m in zip(scores, maxes)]
    outs = [[] for _ in range(qb)]
    for (i, g), e, m in zip(chains, exps, maxes):
        v_t = jnp.concatenate([blocks[i][1], blocks[i + 1][1], blocks[i + 2][1]] + ctx_vt, axis=1)
        lhs = jnp.concatenate([v_t[g * HEAD_DIM:(g + 1) * HEAD_DIM], ones], axis=0).astype(BF16)
        o_t = _dot(lhs, e)
        den = o_t[HEAD_DIM:HEAD_DIM + 1] + jnp.exp2(sinks[g] - m)
        o_t = o_t * (1.0 / den)
        outs[i] += [o_t[:, h * BLOCK:(h + 1) * BLOCK].T[:, :HEAD_DIM] for h in range(grp)]
    for i in range(qb):
        o_ref[i * BLOCK:(i + 1) * BLOCK, :] = jnp.concatenate(outs[i], axis=1).astype(BF16)


def _band_bias():
    grp = N_Q_HEADS // N_KV_HEADS
    kj = jnp.arange(BLOCK)[:, None]
    qi = jnp.arange(BLOCK)[None, :]
    prev = jnp.where(qi <= kj, 0.0, NEG_INF)
    nxt = jnp.where(kj <= qi, 0.0, NEG_INF)
    return jnp.tile(jnp.concatenate([prev, nxt], axis=0).astype(F32), (1, grp))


def _sink_rows(sink):
    grp = N_Q_HEADS // N_KV_HEADS
    row = jnp.repeat(sink.astype(F32).reshape(N_KV_HEADS, grp) * LOG2E, BLOCK, axis=1)
    return jnp.broadcast_to(row[:, None, :], (N_KV_HEADS, 8, grp * BLOCK))


def _sink_cols(sink, nq):
    grp = N_Q_HEADS // N_KV_HEADS
    col = jnp.repeat(sink.astype(F32).reshape(N_KV_HEADS, grp) * LOG2E, nq, axis=1)
    return jnp.broadcast_to(col[:, :, None], (N_KV_HEADS, grp * nq, LANES))


def _latent_attention(p, pc, pc_kv_col, cos_t, sin_t, sink, batch, s, n_ctx):
    nb = s // BLOCK
    grp = N_Q_HEADS // N_KV_HEADS
    qw = N_Q_HEADS * HEAD_DIM
    kvw2 = 2 * N_KV_HEADS * HEAD_DIM
    bias = _band_bias()
    qb = 8 if nb % 8 == 0 else (4 if nb % 4 == 0 else 1)
    ns = nb // qb
    tab = pl.BlockSpec((s, LANES), lambda b, n: (0, 0))
    return pl.pallas_call(
        functools.partial(_attn_kernel, nb=nb, qb=qb),
        grid=(batch, ns),
        in_specs=[pl.BlockSpec((qb * BLOCK, qw), lambda b, n: (b * ns + n, COL_Q // qw)),
                  pl.BlockSpec((s, kvw2), lambda b, n: (b, COL_KV // kvw2)),
                  pl.BlockSpec((n_ctx, kvw2), lambda b, n: (b, pc_kv_col // kvw2)),
                  tab, tab,
                  pl.BlockSpec(bias.shape, lambda b, n: (0, 0)),
                  pl.BlockSpec((N_KV_HEADS, 8, grp * BLOCK), lambda b, n: (0, 0, 0))],
        out_specs=pl.BlockSpec((qb * BLOCK, qw), lambda b, n: (b * ns + n, 0)),
        out_shape=jax.ShapeDtypeStruct((batch * s, qw), BF16),
        compiler_params=_cparams(("parallel", "parallel"), 40),
        name="latent_attention",
    )(p, p, pc, cos_t, sin_t, bias, _sink_rows(sink))


def _cattn_kernel(q_ref, kv_ref, sink_ref, o_ref):
    nq = q_ref.shape[0]
    kvw = N_KV_HEADS * HEAD_DIM
    q = q_ref[...].astype(F32) * (HEAD_DIM ** -0.5 * LOG2E)
    kv = kv_ref[...].astype(F32)
    v_ones = _with_ones(kv[:, kvw:])
    outs = []
    for g in range(N_KV_HEADS):
        qs = _stack_heads(q, g).astype(BF16)
        kh = kv[:, g * HEAD_DIM:(g + 1) * HEAD_DIM].astype(BF16)
        s = lax.dot_general(qs, kh, (((1,), (1,)), ((), ())), preferred_element_type=F32)
        outs.append(_softmax_pv(s, sink_ref[g][:, :1], v_ones, g))
    o_ref[...] = _unstack_heads(outs, nq).astype(BF16)


def _context_attention(pc, sink, batch, n_ctx):
    grp = N_Q_HEADS // N_KV_HEADS
    qw = N_Q_HEADS * HEAD_DIM
    kvw2 = 2 * N_KV_HEADS * HEAD_DIM
    return pl.pallas_call(
        _cattn_kernel,
        grid=(batch,),
        in_specs=[pl.BlockSpec((n_ctx, qw), lambda b: (b, COL_Q // qw)),
                  pl.BlockSpec((n_ctx, kvw2), lambda b: (b, COL_KV // kvw2)),
                  pl.BlockSpec((N_KV_HEADS, grp * n_ctx, LANES), lambda b: (0, 0, 0))],
        out_specs=pl.BlockSpec((n_ctx, qw), lambda b: (b, 0)),
        out_shape=jax.ShapeDtypeStruct((batch * n_ctx, qw), BF16),
        compiler_params=_cparams(("parallel",), 40),
        name="context_attention",
    )(pc, pc, _sink_cols(sink, n_ctx))


FFT_TILE = 16


def _fft1_kernel(v_ref, w_ref, tc_ref, ts_ref, o_ref):
    df = D_FOURIER
    w = w_ref[...]
    n2 = w.shape[0] // 2
    for i in range(FFT_TILE):
        x = v_ref[:, i, :]
        z = jnp.concatenate([x[:, :df], x[:, df:]], axis=0).astype(BF16)
        c = _dot(w, z)
        cr, ci = c[:n2], c[n2:]
        tc = jnp.concatenate([tc_ref[i]] * (df // LANES), axis=1)
        ts = jnp.concatenate([ts_ref[i]] * (df // LANES), axis=1)
        o_ref[i, :, :df] = (cr * tc - ci * ts).astype(BF16)
        o_ref[i, :, df:] = (cr * ts + ci * tc).astype(BF16)


def _fft2_kernel(y_ref, w_ref, o_ref):
    df = D_FOURIER
    w = w_ref[...]
    for i in range(FFT_TILE):
        y = jnp.concatenate([y_ref[:, 2 * i * df:(2 * i + 1) * df],
                             y_ref[:, (2 * i + 1) * df:(2 * i + 2) * df]], axis=0)
        o_ref[:, i, :] = _dot(w, y)


def _fft_split(s):
    n1 = 1 << ((s.bit_length() - 1) // 2)
    return n1, s // n1


def _fft_tables(s):
    n1, n2 = _fft_split(s)
    c2, s2 = _dft_tables(n2)
    w1 = jnp.concatenate([jnp.concatenate([c2, -s2], axis=1),
                          jnp.concatenate([s2, c2], axis=1)], axis=0).astype(BF16)
    c1, s1 = _dft_tables(n1)
    w2 = jnp.concatenate([c1, -s1], axis=1).astype(BF16)
    ang = (jnp.arange(n1)[:, None] * jnp.arange(n2)[None, :]).astype(F32) * (2.0 * math.pi / s)
    tc = jnp.broadcast_to(jnp.cos(ang)[:, :, None], (n1, n2, LANES))
    ts = jnp.broadcast_to(jnp.sin(ang)[:, :, None], (n1, n2, LANES))
    return w1, w2, tc, ts


def _position_dft(v, tables, batch, s):
    w1, w2, tc, ts = tables
    n1, n2 = _fft_split(s)
    df2 = 2 * D_FOURIER
    nt1 = n1 // FFT_TILE
    nt2 = n2 // FFT_TILE
    stage1 = pl.pallas_call(
        _fft1_kernel,
        grid=(batch, nt1),
        in_specs=[pl.BlockSpec((n2, FFT_TILE, df2), lambda b, j: (b, j, 0)),
                  pl.BlockSpec(w1.shape, lambda b, j: (0, 0)),
                  pl.BlockSpec((FFT_TILE, n2, LANES), lambda b, j: (j, 0, 0)),
                  pl.BlockSpec((FFT_TILE, n2, LANES), lambda b, j: (j, 0, 0))],
        out_specs=pl.BlockSpec((FFT_TILE, n2, df2), lambda b, j: (b * nt1 + j, 0, 0)),
        out_shape=jax.ShapeDtypeStruct((batch * n1, n2, df2), BF16),
        compiler_params=_cparams(("parallel", "parallel"), 40),
        name="fft_stage1",
    )(v.reshape(batch * n2, n1, df2), w1, tc, ts)
    out = pl.pallas_call(
        _fft2_kernel,
        grid=(batch, nt2),
        in_specs=[pl.BlockSpec((n1, FFT_TILE * df2), lambda b, j: (b, j)),
                  pl.BlockSpec(w2.shape, lambda b, j: (0, 0))],
        out_specs=pl.BlockSpec((n1, FFT_TILE, D_FOURIER), lambda b, j: (b, j, 0)),
        out_shape=jax.ShapeDtypeStruct((batch * n1, n2, D_FOURIER), F32),
        compiler_params=_cparams(("parallel", "parallel"), 40),
        name="fft_stage2",
    )(stage1.reshape(batch * n1, n2 * df2), w2)
    return out.reshape(batch * s, D_FOURIER)


def _dft_tables(n):
    k = jnp.arange(n, dtype=I32)
    ang = ((k[:, None] * k[None, :]) % n).astype(F32) * (2.0 * math.pi / n)
    scale = n ** -0.5
    return jnp.cos(ang) * scale, jnp.sin(ang) * scale


def _channel_dft_matrix():
    cg = D_FOURIER // N_FOURIER_GROUPS
    c, s = _dft_tables(cg)
    eye = jnp.eye(N_FOURIER_GROUPS, dtype=F32)
    return jnp.concatenate([jnp.kron(eye, c), jnp.kron(eye, s)], axis=1).astype(BF16)


def _merge_kernel(ba_ref, bb_ref, bc_ref, bd_ref, gl_ref, gb_ref, wa_ref, wb_ref, wc_ref, wd_ref,
                  wo_ref, xs_ref, m2_ref, m3_ref, m4_ref, pg_ref, fg_ref, rw_ref,
                  xo_ref, hp_ref, aff_ref, aff_t_ref):
    tm, d = xs_ref.shape
    halves = [pl.ds(i * (tm // MERGE_PARTS), tm // MERGE_PARTS) for i in range(MERGE_PARTS)]
    ys = []
    for rs in halves:
        y = None
        for i, (b_ref, w_ref) in enumerate(((ba_ref, wa_ref), (bb_ref, wb_ref),
                                            (bc_ref, wc_ref), (bd_ref, wd_ref))):
            gate = _tanh_gate(gl_ref[rs, i * d:(i + 1) * d].astype(F32) + gb_ref[:, i * d:(i + 1) * d])
            term = gate * _dot(b_ref[rs, :].astype(BF16), w_ref[...])
            y = term if y is None else y + term
        ys.append(y.astype(BF16))
    zs = [_dot(y, wo_ref[...]) for y in ys]
    hs = []
    for rs, z in zip(halves, zs):
        xs = xs_ref[rs, :] + m2_ref[0] * _rms(z, pg_ref[...])
        xo_ref[rs, :] = xs
        h = _rms(xs, fg_ref[...]) * (1.0 + m4_ref[0]) + m3_ref[0]
        hp_ref[rs, :] = h.astype(BF16).astype(F32)
        hs.append(h)
    for rs, h in zip(halves, hs):
        h_hi, h_lo = _split_bf16(h)
        p_hi = _dot(h_hi, rw_ref[...])
        logits = p_hi + pltpu.roll(p_hi, LANES - N_EXPERTS, 1) + _dot(h_lo, rw_ref[...])
        lane = lax.broadcasted_iota(I32, logits.shape, 1)
        logits = jnp.where(lane < N_EXPERTS, logits, NEG_INF)
        e = jnp.exp(logits - jnp.max(logits, axis=1, keepdims=True))
        aff = e / jnp.sum(e, axis=1, keepdims=True)
        aff_ref[rs, :] = aff
        aff_t_ref[:, rs] = aff.T[:N_EXPERTS]


def _merge(ba, bb, bc, bd, p, gate_b, wa, wb, wc, wd, wo, xs, m2, m3, m4, post_g, ffn_g, rw,
           rows_per_group):
    r, d = xs.shape
    tm = min(512, rows_per_group)
    tiles_per_group = rows_per_group // tm
    gw = N_BRANCH * d

    def rows(width):
        return pl.BlockSpec((tm, width), lambda i: (i, 0))

    def const(shape):
        return pl.BlockSpec(shape, lambda i: (0,) * len(shape))

    mod = pl.BlockSpec((1, 1, d), lambda i: (i // tiles_per_group, 0, 0))
    half = d // 2
    return pl.pallas_call(
        _merge_kernel,
        grid=(r // tm,),
        in_specs=[rows(half), rows(half), rows(half), rows(half),
                  pl.BlockSpec((tm, gw), lambda i: (i, COL_G // gw)),
                  const((1, gw)),
                  const((half, d)), const((half, d)), const((half, d)), const((half, d)),
                  const((d, d)),
                  rows(d), mod, mod, mod, const((1, d)), const((1, d)), const((d, LANES))],
        out_specs=[rows(d), rows(d), rows(LANES),
                   pl.BlockSpec((N_EXPERTS, tm), lambda i: (0, i))],
        out_shape=[jax.ShapeDtypeStruct((r, d), F32),
                   jax.ShapeDtypeStruct((r, d), F32),
                   jax.ShapeDtypeStruct((r, LANES), F32),
                   jax.ShapeDtypeStruct((N_EXPERTS, r), F32)],
        compiler_params=_cparams(("parallel",), 48),
        name="merge_router",
    )(ba, bb, bc, bd, p, 0.5 * gate_b.reshape(1, gw), wa, wb, wc, wd, wo, xs, m2, m3, m4,
      post_g.reshape(1, d), ffn_g.reshape(1, d), rw)


def _route_kernel(aff_ref, aff_t_ref, tri_ref, tl_ref, idx_ref, val_ref, *, cap):
    s = aff_ref.shape[0]
    nslot = idx_ref.shape[-1]
    aff = aff_ref[...]

    def as_float(bits):
        return lax.bitcast_convert_type(bits, F32)

    def count(mask):
        part = jnp.sum(jnp.where(mask, 1.0, 0.0).reshape(s // 64, 64, LANES), axis=0)
        return jnp.sum(part, axis=0, keepdims=True)

    aff_t = aff_t_ref[...]

    def search(i, thr):
        cand = thr | lax.shift_left(jnp.int32(1), 30 - i)
        above = jnp.sum(jnp.where(aff_t >= as_float(cand), 1.0, 0.0), axis=1, keepdims=True)
        return jnp.where(above >= cap, cand, thr)

    thr = lax.fori_loop(0, 31, search, jnp.zeros((N_EXPERTS, 1), I32))

    def as_lane_row(col):
        block = jnp.concatenate([jnp.broadcast_to(col, (N_EXPERTS, LANES)),
                                 jnp.zeros((LANES - N_EXPERTS, LANES), F32)], axis=0)
        return block.T[0:1, :]

    gt = aff >= as_lane_row(as_float(jnp.maximum(thr + 1, MIN_NORMAL_BITS)))
    eq = (aff >= as_lane_row(as_float(thr))) & jnp.logical_not(gt)
    need = cap - count(gt)

    tri = tri_ref[...]

    def cumsum_excl(m):
        off = jnp.zeros((1, LANES), F32)
        outs = []
        for c in range(s // LANES):
            mc = m[c * LANES:(c + 1) * LANES]
            cs = _dot(tri, mc.astype(BF16))
            outs.append(cs - mc + off)
            off = off + cs[LANES - 1:LANES, :]
        return jnp.concatenate(outs, axis=0)

    eq_f = jnp.where(eq, 1.0, 0.0)
    sel = gt | (eq & (cumsum_excl(eq_f) < need))
    sel_f = jnp.where(sel, 1.0, 0.0)
    pos = jnp.where(sel, cumsum_excl(sel_f), -1.0)

    slot = lax.broadcasted_iota(I32, (s, nslot), 1).astype(F32)
    tl = tl_ref[...]
    row = lax.broadcasted_iota(I32, tl.shape, 0)
    vals = []
    for e in range(N_EXPERTS):
        onehot = jnp.where(pos[:, e:e + 1] == slot, 1.0, 0.0).astype(BF16)
        a = aff_t_ref[e:e + 1, :]
        a_hi = a.astype(BF16).astype(F32)
        a_mid = (a - a_hi).astype(BF16).astype(F32)
        a_lo = a - a_hi - a_mid
        lhs = jnp.where(row == 2, a_hi, jnp.where(row == 3, a_mid, jnp.where(row == 4, a_lo, tl)))
        res = _dot(lhs.astype(BF16), onehot)
        idx_ref[0, e:e + 1, :] = (res[0:1] * 64.0 + res[1:2] + 0.5).astype(I32)
        vals.append(res[2:3] + res[3:4] + res[4:5])
    vals = jnp.concatenate(vals + [jnp.zeros((LANES - N_EXPERTS, nslot), F32)], axis=0)
    vals_t = jnp.concatenate([vals[:, c * LANES:(c + 1) * LANES].T for c in range(nslot // LANES)],
                             axis=0)
    val_ref[...] = vals_t[:cap]


def _route(aff, aff_t, batch, s, cap):
    nslot = max(cap, LANES)
    tri = (jnp.arange(LANES)[:, None] >= jnp.arange(LANES)[None, :]).astype(BF16)
    t = jnp.arange(s)
    tl = jnp.zeros((8, s), F32).at[0].set(t // 64).at[1].set(t % 64)
    idx, vals = pl.pallas_call(
        functools.partial(_route_kernel, cap=cap),
        grid=(batch,),
        in_specs=[pl.BlockSpec((s, LANES), lambda b: (b, 0)),
                  pl.BlockSpec((N_EXPERTS, s), lambda b: (0, b)),
                  pl.BlockSpec((LANES, LANES), lambda b: (0, 0)),
                  pl.BlockSpec((8, s), lambda b: (0, 0))],
        out_specs=[pl.BlockSpec((1, N_EXPERTS, nslot), lambda b: (b, 0, 0)),
                   pl.BlockSpec((cap, LANES), lambda b: (b, 0))],
        out_shape=[jax.ShapeDtypeStruct((batch, N_EXPERTS, nslot), I32),
                   jax.ShapeDtypeStruct((batch * cap, LANES), F32)],
        compiler_params=_cparams(("parallel",), 48),
        name="route_topk",
    )(aff, aff_t, tri, tl)
    return idx[:, :, :cap].reshape(-1), vals


def _tile_row(t):
    return lax.shift_right_logical(t, SUBLANES.bit_length() - 1), t & (SUBLANES - 1)


def _experts_per_step(cap):
    return max(1, min(N_EXPERTS, 512 // cap))


def _gather_kernel(idx_ref, h_ref, xg_ref, g_ref, *, cap, eps):
    b = pl.program_id(0)
    first = pl.program_id(1) * eps
    groups = cap // SUBLANES

    for ei in range(eps):
        base = (b * N_EXPERTS + first + ei) * cap

        def body(jg, carry, ei=ei, base=base):
            j0 = pl.multiple_of(jg * SUBLANES, SUBLANES)
            for k in range(SUBLANES):
                t = idx_ref[base + j0 + k]
                hi, lo = _tile_row(t)
                g_ref[ei * groups + jg, pl.ds(k, 1), :] = h_ref[hi, pl.ds(lo, 1), :]
            return carry

        lax.fori_loop(0, groups, body, 0)
    xg_ref[...] = g_ref[...].reshape(xg_ref.shape).astype(BF16)


def _gather(idx, hp, batch, s, cap):
    d = hp.shape[1]
    eps = _experts_per_step(cap)
    grid_spec = pltpu.PrefetchScalarGridSpec(
        num_scalar_prefetch=1,
        grid=(batch, N_EXPERTS // eps),
        in_specs=[pl.BlockSpec((s // SUBLANES, SUBLANES, d), lambda b, e, idx: (b, 0, 0))],
        out_specs=pl.BlockSpec((eps, cap, d), lambda b, e, idx: (e, b, 0)),
        scratch_shapes=[pltpu.VMEM((eps * cap // SUBLANES, SUBLANES, d), F32)])
    xg = pl.pallas_call(
        functools.partial(_gather_kernel, cap=cap, eps=eps),
        grid_spec=grid_spec,
        out_shape=jax.ShapeDtypeStruct((N_EXPERTS, batch * cap, d), BF16),
        compiler_params=_cparams(("arbitrary", "arbitrary"), 48),
        name="moe_gather",
    )(idx, hp.reshape(-1, SUBLANES, d))
    return xg.reshape(N_EXPERTS * batch * cap, d)


def _ffn_kernel(*refs, n_sets):
    ins, rest = refs[:2 * n_sets], refs[2 * n_sets:]
    w1_ref, w3_ref, w2_ref = rest[:3]
    outs = rest[3:3 + n_sets]
    w1b, w3b, w2b = rest[3 + n_sets:]
    expert = pl.program_id(0)

    def run(x_ref, v_ref, y_ref):
        x = x_ref[...]
        hid = _silu(_dot(x, w1b[...])) * _dot(x, w3b[...])
        y = _dot(hid.astype(BF16), w2b[...])
        lane = lax.broadcasted_iota(I32, v_ref.shape, 1)
        v = jnp.sum(jnp.where(lane == expert, v_ref[...], 0.0), axis=1, keepdims=True)
        y_ref[...] = y * v

    @pl.when(pl.program_id(1) == 0)
    def _():
        w1b[...] = w1_ref[0, 0].astype(BF16)
        w3b[...] = w3_ref[0, 0].astype(BF16)
        w2b[...] = w2_ref[0, 0].astype(BF16)
        for k in range(1, n_sets):
            run(ins[2 * k], ins[2 * k + 1], outs[k])

    run(ins[0], ins[1], outs[0])


def _ffn(sets, w1, w3, w2, layer):
    d = sets[0][0].shape[1]
    f = w1.shape[-1]
    tm = min(1024, sets[0][2])
    nt = sets[0][2] // tm

    def wspec(a, c):
        return pl.BlockSpec((1, 1, a, c), lambda e, m: (layer, e, 0, 0))

    in_specs = [pl.BlockSpec((tm, d), lambda e, m: (e * nt + m, 0)),
                pl.BlockSpec((tm, LANES), lambda e, m: (m, 0))]
    out_specs = [pl.BlockSpec((tm, d), lambda e, m: (e * nt + m, 0))]
    operands = [sets[0][0], sets[0][1]]
    for xg, vals, rpe in sets[1:]:
        in_specs += [pl.BlockSpec((rpe, d), lambda e, m: (e, 0)),
                     pl.BlockSpec((rpe, LANES), lambda e, m: (0, 0))]
        out_specs.append(pl.BlockSpec((rpe, d), lambda e, m: (e, 0)))
        operands += [xg, vals]
    return pl.pallas_call(
        functools.partial(_ffn_kernel, n_sets=len(sets)),
        grid=(N_EXPERTS, nt),
        in_specs=in_specs + [wspec(d, f), wspec(d, f), wspec(f, d)],
        out_specs=out_specs,
        out_shape=[jax.ShapeDtypeStruct(xg.shape, F32) for xg, _, _ in sets],
        scratch_shapes=[pltpu.VMEM((d, f), BF16), pltpu.VMEM((d, f), BF16), pltpu.VMEM((f, d), BF16)],
        compiler_params=_cparams(("parallel", "arbitrary"), 56),
        name="expert_ffn",
    )(*operands, w1, w3, w2)


def _combine_kernel(idx_ref, y_ref, xs_ref, m5_ref, g_ref, o_ref, acc_ref, *, cap, tf, eps):
    b = pl.program_id(0)
    step = pl.program_id(1)
    scatter_steps = N_EXPERTS // eps

    @pl.when(step == 0)
    def _():
        acc_ref[...] = jnp.zeros_like(acc_ref)

    @pl.when(step < scatter_steps)
    def _():
        for ei in range(eps):
            base = (b * N_EXPERTS + step * eps + ei) * cap

            def body(jg, carry, ei=ei, base=base):
                j0 = pl.multiple_of(jg * SUBLANES, SUBLANES)
                toks = [_tile_row(idx_ref[base + j0 + k]) for k in range(SUBLANES)]
                rows = [acc_ref[hi, pl.ds(lo, 1), :] for hi, lo in toks]
                for k, (hi, lo) in enumerate(toks):
                    acc_ref[hi, pl.ds(lo, 1), :] = rows[k] + y_ref[ei, jg, pl.ds(k, 1), :]
                return carry

            lax.fori_loop(0, cap // SUBLANES, body, 0)

    @pl.when(step >= scatter_steps)
    def _():
        r0 = pl.multiple_of((step - scatter_steps) * (tf // SUBLANES), tf // SUBLANES)
        moe = acc_ref[pl.ds(r0, tf // SUBLANES)].reshape(o_ref.shape)
        o_ref[...] = xs_ref[...] + m5_ref[0] * _rms(moe, g_ref[...])


def _combine(idx, y, xs, m5, post_g, batch, s, cap, shared_mod):
    d = xs.shape[1]
    tf = min(1024, s)
    nfin = s // tf
    eps = _experts_per_step(cap)
    scatter_steps = N_EXPERTS // eps

    def chunk_map(b, st, idx):
        return (b * nfin + jnp.maximum(st - scatter_steps, 0), 0)

    grid_spec = pltpu.PrefetchScalarGridSpec(
        num_scalar_prefetch=1,
        grid=(batch, scatter_steps + nfin),
        in_specs=[pl.BlockSpec((eps, cap // SUBLANES, SUBLANES, d),
                               lambda b, st, idx: (jnp.minimum(st, scatter_steps - 1), b, 0, 0)),
                  pl.BlockSpec((tf, d), chunk_map),
                  pl.BlockSpec((1, 1, d), lambda b, st, idx: (0 if shared_mod else b, 0, 0)),
                  pl.BlockSpec((1, d), lambda b, st, idx: (0, 0))],
        out_specs=pl.BlockSpec((tf, d), chunk_map),
        scratch_shapes=[pltpu.VMEM((s // SUBLANES, SUBLANES, d), F32)])
    return pl.pallas_call(
        functools.partial(_combine_kernel, cap=cap, tf=tf, eps=eps),
        grid_spec=grid_spec,
        out_shape=jax.ShapeDtypeStruct(xs.shape, F32),
        compiler_params=_cparams(("arbitrary", "arbitrary"), 48),
        name="moe_combine",
    )(idx, y.reshape(N_EXPERTS, -1, SUBLANES, d), xs, m5, post_g.reshape(1, d))


def _prep_w_in(w):
    d = w.shape[0]
    kv = 2 * N_KV_HEADS * HEAD_DIM
    o_q = 2 * D_CONV + 3 * D_SHORT
    o_k = o_q + N_Q_HEADS * HEAD_DIM
    o_f = o_k + kv
    o_g = o_f + D_FOURIER
    parts = [w[:, :o_k], w[:, o_f:o_g], w[:, o_k:o_f],
             jnp.zeros((d, COL_G - COL_KV - kv), w.dtype), 0.5 * w[:, o_g:]]
    return jnp.concatenate(parts, axis=1).astype(BF16)


def _rope_tables(s):
    t = jnp.arange(s)
    row = (t // GRID_W).astype(F32)
    col = (t % GRID_W).astype(F32)
    nf = HEAD_DIM // 4
    inv = ROPE_BASE ** (-jnp.arange(nf, dtype=F32) / nf)
    ar = row[:, None] * inv
    ac = col[:, None] * inv
    cos = jnp.concatenate([jnp.cos(ar), jnp.cos(ar), jnp.cos(ac), jnp.cos(ac)], axis=1)
    sin = jnp.concatenate([-jnp.sin(ar), jnp.sin(ar), -jnp.sin(ac), jnp.sin(ac)], axis=1)
    rep = LANES // HEAD_DIM
    return jnp.tile(cos, (1, rep)), jnp.tile(sin, (1, rep))


def _dispatch(routed, batch, s):
    hp, aff, aff_t = routed
    cap = CAPACITY_FACTOR * s // N_EXPERTS
    idx, vals = _route(aff, aff_t, batch, s, cap)
    return idx, (_gather(idx, hp, batch, s, cap), vals, batch * cap)


def kernel(x, c, ctx, c_ctx, ada_w, ada_b, pre_mix_g, post_mix_g, pre_ffn_g, post_ffn_g, w_in, gate_b, conv_a_w, conv_a_b, ln_a_g, ln_a_b, w_a_out, conv_b_w, w_b_out, sink, w_c_out, w_d_out, w_o, router_w, exp_w1, exp_w3, exp_w2):
    batch, s, d = x.shape
    n_ctx = ctx.shape[1]
    depth = ada_w.shape[0]

    cvec = jnp.zeros((16, d), F32).at[:batch].set(c).at[batch].set(c_ctx)
    mod = _ada(cvec, ada_w, ada_b)
    cos_t, sin_t = _rope_tables(s)
    bd = _channel_dft_matrix()
    dft_x = _fft_tables(s)
    dft_c = _fft_tables(n_ctx)

    xs = x.reshape(batch * s, d)
    cs = ctx.reshape(batch * n_ctx, d)
    for l in range(depth):
        last = l == depth - 1
        mx = [mod[l, :batch, k * d:(k + 1) * d].reshape(batch, 1, d) for k in range(6)]
        mc = [mod[l, batch:batch + 1, k * d:(k + 1) * d].reshape(1, 1, d) for k in range(6)]
        g_pre = pre_mix_g[l].reshape(1, d)
        w = _prep_w_in(w_in[l])
        wa, wb, wc, wd = ((0.5 * t[l]).astype(BF16) for t in (w_a_out, w_b_out, w_c_out, w_d_out))
        wo = w_o[l].astype(BF16)
        r_hi = router_w[l].astype(BF16)
        r_lo = (router_w[l] - r_hi.astype(F32)).astype(BF16)
        rw = (jnp.zeros((d, LANES), BF16).at[:, :N_EXPERTS].set(r_hi)
              .at[:, N_EXPERTS:2 * N_EXPERTS].set(r_lo))

        p = _inproj(xs, mx[0], mx[1], g_pre, w, s)
        if last:
            kv_tile = COL_KV // INPROJ_TN * INPROJ_TN
            pc = _inproj(cs, mc[0], mc[1], g_pre, w[:, kv_tile:kv_tile + INPROJ_TN], batch * n_ctx)
            pc_kv_col = COL_KV - kv_tile
        else:
            pc = _inproj(cs, mc[0], mc[1], g_pre, w, batch * n_ctx)
            pc_kv_col = COL_KV

        def mixer(pp, att, seq, tables, xres, m, rows_per_group):
            ba = _conformer(pp, conv_a_w[l], conv_a_b[l], ln_a_g[l], ln_a_b[l], batch, seq)
            bb = _short_conv(pp, conv_b_w[l], batch, seq)
            v = _mm(pp, bd, a_cols=COL_F, out_dtype=F32)
            bf = _position_dft(v, tables, batch, seq)
            return _merge(ba, bb, att, bf, pp, gate_b[l], wa, wb, wc, wd, wo, xres,
                          m[2], m[3], m[4], post_mix_g[l], pre_ffn_g[l], rw, rows_per_group)

        att_x = _latent_attention(p, pc, pc_kv_col, cos_t, sin_t, sink[l], batch, s, n_ctx)
        xs, *routed = mixer(p, att_x, s, dft_x, xs, mx, s)
        idx_x, set_x = _dispatch(routed, batch, s)
        cap_x = CAPACITY_FACTOR * s // N_EXPERTS
        if last:
            y_x, = _ffn([set_x], exp_w1, exp_w3, exp_w2, l)
        else:
            att_c = _context_attention(pc, sink[l], batch, n_ctx)
            cs, *routed_c = mixer(pc, att_c, n_ctx, dft_c, cs, mc, batch * n_ctx)
            idx_c, set_c = _dispatch(routed_c, batch, n_ctx)
            y_x, y_c = _ffn([set_x, set_c], exp_w1, exp_w3, exp_w2, l)
            cs = _combine(idx_c, y_c, cs, mc[5], post_ffn_g[l], batch, n_ctx,
                          CAPACITY_FACTOR * n_ctx // N_EXPERTS, True)
        xs = _combine(idx_x, y_x, xs, mx[5], post_ffn_g[l], batch, s, cap_x, False)
    return xs.reshape(batch, s, d)
```
